```python
import math
import jax, jax.numpy as jnp
from jax import lax
import numpy as np

D_MODEL = 1024
BATCH = 16
SEQ = 2048
DEPTH = 2

D_MIX = D_MODEL
DIFF_HEADS = 4
DIFF_QK_DIM = D_MIX // 16
DIFF_V_DIM = 2 * DIFF_QK_DIM
RET_HEADS = 4
RET_K_DIM = D_MIX // 16
RET_V_DIM = D_MIX // 8
D_FF = ((8 * D_MODEL // 3 + 255) // 256) * 256
Q_BLOCK = 128
RET_CHUNK = 128
NORM_EPS = 1e-6
HALF_STEP = 0.5
LAMBDA_INIT_STD = 0.1

DIFF_QK_W = DIFF_HEADS * 2 * DIFF_QK_DIM
DIFF_V_W = DIFF_HEADS * DIFF_V_DIM
RET_QK_W = RET_HEADS * RET_K_DIM
RET_V_W = RET_HEADS * RET_V_DIM
IN_COLS = 2 * DIFF_QK_W + DIFF_V_W + 2 * RET_QK_W + 2 * RET_V_W
OUT_ROWS = DIFF_V_W + RET_V_W

kernel_name = 'hymba_diffattn_retention_macaron'


def rmsnorm(x, g):
    xf = x.astype(jnp.float32)
    y = xf * lax.rsqrt(jnp.mean(xf * xf, axis=-1, keepdims=True) + NORM_EPS)
    return (y * g.astype(jnp.float32)).astype(x.dtype)


def head_rms(x):
    return x * lax.rsqrt(jnp.mean(x * x, axis=-1, keepdims=True) + NORM_EPS)


def swiglu(x, w_gate, w_up, w_down):
    return (jax.nn.silu(x @ w_gate) * (x @ w_up)) @ w_down


def alibi_slopes(n_heads):
    return jnp.asarray(2.0 ** (-8.0 * np.arange(1, n_heads + 1) / n_heads), dtype=jnp.float32)


def retention_log_decay(n_heads):
    return jnp.asarray(np.log(1.0 - 2.0 ** (-5.0 - np.arange(n_heads))), dtype=jnp.float32)


def diff_heads(t, B, S):
    return jnp.transpose(t.reshape(B, S, DIFF_HEADS, 2, DIFF_QK_DIM), (0, 2, 3, 1, 4))


def diff_attention(q, k, v, lam, lam_init, subln_g):
    B, H, _, S, d = q.shape
    dv = v.shape[-1]
    nb = S // Q_BLOCK
    scale = d ** -0.5
    slopes = alibi_slopes(H)
    key_pos = jnp.arange(S)
    q_blocks = jnp.moveaxis(q.reshape(B, H, 2, nb, Q_BLOCK, d), 3, 0)
    g = subln_g.astype(jnp.float32)

    def block(args):
        idx, qb = args
        q_pos = idx * Q_BLOCK + jnp.arange(Q_BLOCK)
        dist = (q_pos[:, None] - key_pos[None, :]).astype(jnp.float32)
        bias = -slopes[:, None, None] * dist
        s = jnp.einsum('bhcqd,bhckd->bhcqk', qb, k).astype(jnp.float32) * scale + bias[None, :, None]
        s = jnp.where(dist >= 0, s, -jnp.inf)
        p = jax.nn.softmax(s, axis=-1)
        probs = p[:, :, 0] - lam * p[:, :, 1]
        o = jnp.einsum('bhqk,bhkv->bhqv', probs.astype(v.dtype), v).astype(jnp.float32)
        return head_rms(o) * g * (1.0 - lam_init)

    out = lax.map(block, (jnp.arange(nb), q_blocks))
    return jnp.transpose(out, (1, 0, 3, 2, 4)).reshape(B, S, H * dv)


def retention(q, k, v):
    B, S, H, dk = q.shape
    dv = v.shape[-1]
    C = RET_CHUNK
    N = S // C
    lg = retention_log_decay(H)
    q = q.reshape(B, N, C, H, dk)
    k = k.reshape(B, N, C, H, dk) * (dk ** -0.5)
    v = v.reshape(B, N, C, H, dv)
    pos = jnp.arange(C, dtype=jnp.float32)
    rel = pos[:, None] - pos[None, :]
    intra = jnp.where(rel >= 0, jnp.exp(lg[:, None, None] * rel), 0.0)
    scores = jnp.einsum('bnihd,bnjhd->bnhij', q, k) * intra
    inner = jnp.einsum('bnhij,bnjhv->bnihv', scores, v)
    tail = jnp.exp(lg[:, None] * (C - 1.0 - pos))
    kv = jnp.einsum('bnjhd,bnjhv,hj->nbhdv', k, v, tail)
    chunk_decay = jnp.exp(lg * C)[None, :, None, None]

    def step(state, kv_n):
        return state * chunk_decay + kv_n, state

    _, prev = lax.scan(step, jnp.zeros(kv.shape[1:], kv.dtype), kv)
    head = jnp.exp(lg[:, None] * (pos + 1.0))
    cross = jnp.einsum('bnihd,nbhdv,hi->bnihv', q, prev, head)
    o = (inner + cross).astype(jnp.float32)
    return head_rms(o).reshape(B, S, H * dv)


def setup_inputs(seed: int = 0) -> dict:
    key = jax.random.key(seed)
    ks = jax.random.split(key, 20)
    f32 = jnp.float32

    def normal(k, shape, scale):
        return jax.random.normal(k, shape, f32) * scale

    def gain(k, shape):
        return 1.0 + 0.02 * jax.random.normal(k, shape, f32)

    return {
        'x': normal(ks[0], (BATCH, SEQ, D_MODEL), 1.0),
        'norm_ffn1': gain(ks[1], (DEPTH, D_MODEL)),
        'w1_gate': normal(ks[2], (DEPTH, D_MODEL, D_FF), D_MODEL ** -0.5),
        'w1_up': normal(ks[3], (DEPTH, D_MODEL, D_FF), D_MODEL ** -0.5),
        'w1_down': normal(ks[4], (DEPTH, D_FF, D_MODEL), D_FF ** -0.5),
        'norm_mix': gain(ks[5], (DEPTH, D_MODEL)),
        'w_in': normal(ks[6], (DEPTH, D_MODEL, IN_COLS), D_MODEL ** -0.5),
        'lambda_q1': normal(ks[7], (DEPTH, DIFF_QK_DIM), LAMBDA_INIT_STD),
        'lambda_k1': normal(ks[8], (DEPTH, DIFF_QK_DIM), LAMBDA_INIT_STD),
        'lambda_q2': normal(ks[9], (DEPTH, DIFF_QK_DIM), LAMBDA_INIT_STD),
        'lambda_k2': normal(ks[10], (DEPTH, DIFF_QK_DIM), LAMBDA_INIT_STD),
        'diff_subln': gain(ks[11], (DEPTH, DIFF_V_DIM)),
        'w_out': normal(ks[12], (DEPTH, OUT_ROWS, D_MODEL), OUT_ROWS ** -0.5),
        'norm_ffn2': gain(ks[13], (DEPTH, D_MODEL)),
        'w2_gate': normal(ks[14], (DEPTH, D_MODEL, D_FF), D_MODEL ** -0.5),
        'w2_up': normal(ks[15], (DEPTH, D_MODEL, D_FF), D_MODEL ** -0.5),
        'w2_down': normal(ks[16], (DEPTH, D_FF, D_MODEL), D_FF ** -0.5),
        'norm_final': gain(ks[17], (D_MODEL,)),
    }


def reference(x, norm_ffn1, w1_gate, w1_up, w1_down, norm_mix, w_in, lambda_q1, lambda_k1, lambda_q2, lambda_k2, diff_subln, w_out, norm_ffn2, w2_gate, w2_up, w2_down, norm_final):
    B, S, _ = x.shape
    split_at = np.cumsum([DIFF_QK_W, DIFF_QK_W, DIFF_V_W, RET_QK_W, RET_QK_W, RET_V_W]).tolist()
    for l in range(DEPTH):
        x = x + HALF_STEP * swiglu(rmsnorm(x, norm_ffn1[l]), w1_gate[l], w1_up[l], w1_down[l])
        h = rmsnorm(x, norm_mix[l])
        proj = h @ w_in[l]
        dq, dk, dv, rq, rk, rv, rg = jnp.split(proj, split_at, axis=-1)
        lam_init = 0.8 - 0.6 * math.exp(-0.3 * l)
        lam = (jnp.exp(jnp.sum(lambda_q1[l].astype(jnp.float32) * lambda_k1[l].astype(jnp.float32)))
               - jnp.exp(jnp.sum(lambda_q2[l].astype(jnp.float32) * lambda_k2[l].astype(jnp.float32)))
               + lam_init)
        v_heads = jnp.transpose(dv.reshape(B, S, DIFF_HEADS, DIFF_V_DIM), (0, 2, 1, 3))
        o_diff = diff_attention(diff_heads(dq, B, S), diff_heads(dk, B, S), v_heads, lam, lam_init, diff_subln[l])
        o_ret = retention(rq.reshape(B, S, RET_HEADS, RET_K_DIM),
                          rk.reshape(B, S, RET_HEADS, RET_K_DIM),
                          rv.reshape(B, S, RET_HEADS, RET_V_DIM))
        o_ret = jax.nn.silu(rg.astype(jnp.float32)) * o_ret
        mixed = jnp.concatenate([o_diff, o_ret], axis=-1).astype(x.dtype)
        x = x + mixed @ w_out[l]
        x = x + HALF_STEP * swiglu(rmsnorm(x, norm_ffn2[l]), w2_gate[l], w2_up[l], w2_down[l])
    return rmsnorm(x, norm_final)
```

```python
import functools
import math

import numpy as np
import jax
import jax.numpy as jnp
from jax import lax
from jax.experimental import pallas as pl
from jax.experimental.pallas import tpu as pltpu

F32 = jnp.float32
BF16 = jnp.bfloat16

D_MODEL = 1024
DEPTH = 2
DIFF_HEADS = 4
DIFF_QK_DIM = 64
DIFF_V_DIM = 128
RET_HEADS = 4
RET_K_DIM = 64
RET_V_DIM = 128
D_FF = 2816
RET_CHUNK = 128
NORM_EPS = 1e-6
HALF_STEP = 0.5

DIFF_QK_W = DIFF_HEADS * 2 * DIFF_QK_DIM
DIFF_V_W = DIFF_HEADS * DIFF_V_DIM
RET_QK_W = RET_HEADS * RET_K_DIM
RET_V_W = RET_HEADS * RET_V_DIM
QKV_COLS = 2 * DIFF_QK_W + DIFF_V_W + 2 * RET_QK_W + RET_V_W
IN_COLS = QKV_COLS + RET_V_W

LANES = 128
MXU_COLS = 256

DQ_BLK = 0
DK_BLK = DIFF_QK_W // LANES
DV_BLK = 2 * DIFF_QK_W // LANES
RQ_BLK = (2 * DIFF_QK_W + DIFF_V_W) // RET_QK_W
RK_BLK = RQ_BLK + 1
RV_BLK = (2 * DIFF_QK_W + DIFF_V_W + 2 * RET_QK_W) // RET_V_W

TOKEN_TILE = 512
FF_CHUNKS = ((0, 1536), (1536, 2816))
ATT_BLOCK = 256
VMEM_LIMIT = 56 * 1024 * 1024

RET_LOG_DECAY = [float(v) for v in
                 np.log(1.0 - 2.0 ** (-5.0 - np.arange(RET_HEADS))).astype(np.float32)]
ALIBI_SLOPES = [float(2.0 ** (-8.0 * (h + 1) / DIFF_HEADS)) for h in range(DIFF_HEADS)]


def _rms(x, g):
    ms = jnp.mean(x * x, axis=-1, keepdims=True)
    return x * lax.rsqrt(ms + NORM_EPS) * g


def _dot(a, b):
    return jnp.dot(a, b, preferred_element_type=F32)


def _dot_nt(a, b):
    return lax.dot_general(a, b, (((1,), (1,)), ((), ())), preferred_element_type=F32)


def _dot_tn(a, b):
    return lax.dot_general(a, b, (((0,), (0,)), ((), ())), preferred_element_type=F32)


def _swiglu_residual(x, g_ref, wg_ref, wu_ref, wd_ref):
    xn = _rms(x, g_ref[...]).astype(BF16)
    y = None
    for c0, c1 in FF_CHUNKS:
        gate = _dot(xn, wg_ref[:, c0:c1])
        up = _dot(xn, wu_ref[:, c0:c1])
        h = (jax.nn.silu(gate) * up).astype(BF16)
        part = _dot(h, wd_ref[c0:c1, :])
        y = part if y is None else y + part
    return x + HALF_STEP * y


def _ffn_kernel(x_ref, g_ref, wg_ref, wu_ref, wd_ref, o_ref):
    o_ref[...] = _swiglu_residual(x_ref[...], g_ref, wg_ref, wu_ref, wd_ref)


def _mix_ffn_kernel(x_ref, od_ref, or_ref, woa_ref, wob_ref, g_ref, wg_ref, wu_ref, wd_ref,
                    gf_ref, o_ref, *, final_norm):
    x = x_ref[...] + _dot(od_ref[...], woa_ref[...]) + _dot(or_ref[...], wob_ref[...])
    x = _swiglu_residual(x, g_ref, wg_ref, wu_ref, wd_ref)
    if final_norm:
        x = _rms(x, gf_ref[...])
    o_ref[...] = x


def _inproj_kernel(x_ref, g_ref, w_ref, qkv_ref, rg_ref):
    xn = _rms(x_ref[...], g_ref[...]).astype(BF16)
    qkv_ref[...] = _dot(xn, w_ref[:, :QKV_COLS]).astype(BF16)
    rg_ref[...] = _dot(xn, w_ref[:, QKV_COLS:])


def _resident(shape):
    return pl.BlockSpec(shape, lambda *_: (0,) * len(shape), pipeline_mode=pl.Buffered(1))


def _ffn_call(x, g, wg, wu, wd):
    t = x.shape[0]
    tile = pl.BlockSpec((TOKEN_TILE, D_MODEL), lambda i: (i, 0))
    return pl.pallas_call(
        _ffn_kernel,
        grid=(t // TOKEN_TILE,),
        in_specs=[tile, _resident((1, D_MODEL)), _resident((D_MODEL, D_FF)),
                  _resident((D_MODEL, D_FF)), _resident((D_FF, D_MODEL))],
        out_specs=tile,
        out_shape=jax.ShapeDtypeStruct((t, D_MODEL), F32),
        compiler_params=pltpu.CompilerParams(
            dimension_semantics=("arbitrary",), vmem_limit_bytes=VMEM_LIMIT),
        name="ffn",
    )(x, g, wg, wu, wd)


def _mix_ffn_call(x, od, orr, woa, wob, g, wg, wu, wd, gf, final_norm):
    t = x.shape[0]
    tile = pl.BlockSpec((TOKEN_TILE, D_MODEL), lambda i: (i, 0))
    half = pl.BlockSpec((TOKEN_TILE, DIFF_V_W), lambda i: (i, 0))
    return pl.pallas_call(
        functools.partial(_mix_ffn_kernel, final_norm=final_norm),
        grid=(t // TOKEN_TILE,),
        in_specs=[tile, half, half,
                  _resident((DIFF_V_W, D_MODEL)), _resident((RET_V_W, D_MODEL)),
                  _resident((1, D_MODEL)), _resident((D_MODEL, D_FF)),
                  _resident((D_MODEL, D_FF)), _resident((D_FF, D_MODEL)),
                  _resident((1, D_MODEL))],
        out_specs=tile,
        out_shape=jax.ShapeDtypeStruct((t, D_MODEL), F32),
        compiler_params=pltpu.CompilerParams(
            dimension_semantics=("arbitrary",), vmem_limit_bytes=VMEM_LIMIT),
        name="mix_ffn",
    )(x, od, orr, woa, wob, g, wg, wu, wd, gf)


def _inproj_call(x, g, w):
    t = x.shape[0]
    return pl.pallas_call(
        _inproj_kernel,
        grid=(t // TOKEN_TILE,),
        in_specs=[pl.BlockSpec((TOKEN_TILE, D_MODEL), lambda i: (i, 0)),
                  _resident((1, D_MODEL)), _resident((D_MODEL, IN_COLS))],
        out_specs=[pl.BlockSpec((TOKEN_TILE, QKV_COLS), lambda i: (i, 0)),
                   pl.BlockSpec((TOKEN_TILE, RET_V_W), lambda i: (i, 0))],
        out_shape=[jax.ShapeDtypeStruct((t, QKV_COLS), BF16),
                   jax.ShapeDtypeStruct((t, RET_V_W), F32)],
        compiler_params=pltpu.CompilerParams(
            dimension_semantics=("arbitrary",), vmem_limit_bytes=VMEM_LIMIT),
        name="inproj",
    )(x, g, w)


def _attn_kernel(lq1_ref, lk1_ref, lq2_ref, lk2_ref, sub_ref, q_ref, k_ref, v_ref, o_ref,
                 q2_s, m_s, l_s, acc_s, *, lam_init):
    bq = ATT_BLOCK
    h = pl.program_id(1)
    i = pl.program_id(2)
    slope = jnp.where(h == 0, ALIBI_SLOPES[0],
                      jnp.where(h == 1, ALIBI_SLOPES[1],
                                jnp.where(h == 2, ALIBI_SLOPES[2], ALIBI_SLOPES[3]))).astype(F32)

    q = q_ref[0].astype(F32) * (DIFF_QK_DIM ** -0.5)
    lane = lax.broadcasted_iota(jnp.int32, (bq, LANES), 1)
    q2_s[0:bq, :] = jnp.where(lane < DIFF_QK_DIM, q, 0.0).astype(BF16)
    q2_s[bq:2 * bq, :] = jnp.where(lane >= DIFF_QK_DIM, q, 0.0).astype(BF16)

    cols = lax.broadcasted_iota(jnp.int32, (1, bq), 1)

    def scores(j):
        off = pl.multiple_of(j * bq, bq)
        k = k_ref[0, pl.ds(off, bq), :]
        v = v_ref[0, pl.ds(off, bq), :]
        s = _dot_nt(q2_s[...], k)
        bias = slope * (cols + (j - i) * bq).astype(F32)
        return s + bias, v

    def split(s):
        return [s[:, c * LANES:(c + 1) * LANES] for c in range(bq // LANES)]

    s, v = scores(i)
    rows = lax.broadcasted_iota(jnp.int32, (2 * bq, bq), 0)
    rows = jnp.where(rows >= bq, rows - bq, rows)
    ccols = lax.broadcasted_iota(jnp.int32, (2 * bq, bq), 1)
    s = jnp.where(rows >= ccols, s, -jnp.inf)
    m0 = jnp.broadcast_to(jnp.max(s, axis=-1, keepdims=True), (2 * bq, LANES))
    ps = [jnp.exp(c - m0) for c in split(s)]
    m_s[...] = m0
    l_s[...] = functools.reduce(lambda a, b: a + b, ps)
    acc_s[...] = _dot(jnp.concatenate(ps, axis=1).astype(BF16), v)

    def body(j, carry):
        s, v = scores(j)
        m_prev = m_s[...]
        m_new = jnp.maximum(m_prev, jnp.max(s, axis=-1, keepdims=True))
        alpha = jnp.exp(m_prev - m_new)
        ps = [jnp.exp(c - m_new) for c in split(s)]
        l_s[...] = alpha * l_s[...] + functools.reduce(lambda a, b: a + b, ps)
        acc_s[...] = alpha * acc_s[...] + _dot(jnp.concatenate(ps, axis=1).astype(BF16), v)
        m_s[...] = m_new
        return carry

    lax.fori_loop(0, i, body, 0)

    lam = (jnp.exp(jnp.sum(lq1_ref[...] * lk1_ref[...], axis=-1, keepdims=True))
           - jnp.exp(jnp.sum(lq2_ref[...] * lk2_ref[...], axis=-1, keepdims=True))
           + lam_init)
    l = jnp.sum(l_s[...], axis=-1, keepdims=True)
    on = acc_s[...] / l
    o = on[0:bq, :] - lam * on[bq:2 * bq, :]
    o = o * lax.rsqrt(jnp.mean(o * o, axis=-1, keepdims=True) + NORM_EPS)
    o_ref[0] = (o * sub_ref[...] * (1.0 - lam_init)).astype(BF16)


def _attn_call(qkv, lq1, lk1, lq2, lk2, sub, lam_init):
    b, s, _ = qkv.shape
    bq = ATT_BLOCK
    vec = lambda n: pl.BlockSpec((1, n), lambda bi, hi, qi: (0, 0))
    return pl.pallas_call(
        functools.partial(_attn_kernel, lam_init=lam_init),
        grid=(b, DIFF_HEADS, s // bq),
        in_specs=[vec(DIFF_QK_DIM), vec(DIFF_QK_DIM), vec(DIFF_QK_DIM), vec(DIFF_QK_DIM),
                  vec(DIFF_V_DIM),
                  pl.BlockSpec((1, bq, LANES), lambda bi, hi, qi: (bi, qi, DQ_BLK + hi)),
                  pl.BlockSpec((1, s, LANES), lambda bi, hi, qi: (bi, 0, DK_BLK + hi)),
                  pl.BlockSpec((1, s, LANES), lambda bi, hi, qi: (bi, 0, DV_BLK + hi))],
        out_specs=pl.BlockSpec((1, bq, DIFF_V_DIM), lambda bi, hi, qi: (bi, qi, hi)),
        out_shape=jax.ShapeDtypeStruct((b, s, DIFF_V_W), BF16),
        scratch_shapes=[pltpu.VMEM((2 * bq, LANES), BF16),
                        pltpu.VMEM((2 * bq, LANES), F32),
                        pltpu.VMEM((2 * bq, LANES), F32),
                        pltpu.VMEM((2 * bq, DIFF_V_DIM), F32)],
        compiler_params=pltpu.CompilerParams(
            dimension_semantics=("arbitrary", "arbitrary", "arbitrary"),
            vmem_limit_bytes=VMEM_LIMIT),
        name="diff_attn",
    )(lq1, lk1, lq2, lk2, sub, qkv, qkv, qkv)


def _ret_kernel(q_ref, k_ref, v_ref, g_ref, o_ref, st_s, dec_s):
    c = RET_CHUNK
    n_chunks = q_ref.shape[1] // c
    r = lax.broadcasted_iota(jnp.int32, (c, c), 0)
    cc = lax.broadcasted_iota(jnp.int32, (c, c), 1)
    rel = (r - cc).astype(F32)
    rf = r.astype(F32)
    for h in range(RET_HEADS):
        lg = RET_LOG_DECAY[h]
        dec_s[h, 0] = jnp.where(rel >= 0, jnp.exp(lg * rel), 0.0)
        dec_s[h, 1] = jnp.exp(lg * (c - 1.0 - rf))
        dec_s[h, 2] = jnp.exp(lg * (rf + 1.0))
        st_s[h] = jnp.zeros((c, c), F32)
    upper = cc >= RET_K_DIM

    def body(n, carry):
        off = pl.multiple_of(n * c, c)
        for p in range(RET_HEADS // 2):
            qp = q_ref[0, pl.ds(off, c), p * LANES:(p + 1) * LANES].astype(F32)
            kp = k_ref[0, pl.ds(off, c), p * LANES:(p + 1) * LANES].astype(F32) * (RET_K_DIM ** -0.5)
            for e in range(2):
                h = 2 * p + e
                sel = upper if e else jnp.logical_not(upper)
                qh = jnp.where(sel, qp, 0.0)
                kh = jnp.where(sel, kp, 0.0)
                v = v_ref[0, pl.ds(off, c), h * RET_V_DIM:(h + 1) * RET_V_DIM]
                sc = _dot_nt(qh.astype(BF16), kh.astype(BF16)) * dec_s[h, 0]
                state = st_s[h]
                o = (_dot(sc.astype(BF16), v)
                     + _dot((qh * dec_s[h, 2]).astype(BF16), state.astype(BF16)))
                o = o * lax.rsqrt(jnp.mean(o * o, axis=-1, keepdims=True) + NORM_EPS)
                gate = g_ref[0, pl.ds(off, c), h * RET_V_DIM:(h + 1) * RET_V_DIM]
                o_ref[0, pl.ds(off, c), h * RET_V_DIM:(h + 1) * RET_V_DIM] = (
                    jax.nn.silu(gate) * o).astype(BF16)
                kv = _dot_tn((kh * dec_s[h, 1]).astype(BF16), v)
                st_s[h] = state * math.exp(RET_LOG_DECAY[h] * c) + kv
        return carry

    lax.fori_loop(0, n_chunks, body, 0)


def _ret_call(qkv, rg):
    b, s, _ = qkv.shape
    return pl.pallas_call(
        _ret_kernel,
        grid=(b,),
        in_specs=[pl.BlockSpec((1, s, RET_QK_W), lambda bi: (bi, 0, RQ_BLK)),
                  pl.BlockSpec((1, s, RET_QK_W), lambda bi: (bi, 0, RK_BLK)),
                  pl.BlockSpec((1, s, RET_V_W), lambda bi: (bi, 0, RV_BLK)),
                  pl.BlockSpec((1, s, RET_V_W), lambda bi: (bi, 0, 0))],
        out_specs=pl.BlockSpec((1, s, RET_V_W), lambda bi: (bi, 0, 0)),
        out_shape=jax.ShapeDtypeStruct((b, s, RET_V_W), BF16),
        scratch_shapes=[pltpu.VMEM((RET_HEADS, RET_CHUNK, RET_CHUNK), F32),
                        pltpu.VMEM((RET_HEADS, 3, RET_CHUNK, RET_CHUNK), F32)],
        compiler_params=pltpu.CompilerParams(
            dimension_semantics=("arbitrary",), vmem_limit_bytes=VMEM_LIMIT),
        name="retention",
    )(qkv, qkv, qkv, rg)


def kernel(x, norm_ffn1, w1_gate, w1_up, w1_down, norm_mix, w_in, lambda_q1, lambda_k1, lambda_q2, lambda_k2, diff_subln, w_out, norm_ffn2, w2_gate, w2_up, w2_down, norm_final):
    b, s, d = x.shape
    t = b * s
    xt = x.reshape(t, d)
    row = lambda a: a.reshape(1, -1).astype(F32)
    for l in range(DEPTH):
        lam_init = 0.8 - 0.6 * math.exp(-0.3 * l)
        xt = _ffn_call(xt, row(norm_ffn1[l]), w1_gate[l].astype(BF16), w1_up[l].astype(BF16),
                       w1_down[l].astype(BF16))
        qkv, rg = _inproj_call(xt, row(norm_mix[l]), w_in[l].astype(BF16))
        qkv = qkv.reshape(b, s, QKV_COLS)
        od = _attn_call(qkv, row(lambda_q1[l]), row(lambda_k1[l]), row(lambda_q2[l]),
                        row(lambda_k2[l]), row(diff_subln[l]), lam_init)
        orr = _ret_call(qkv, rg.reshape(b, s, RET_V_W))
        wo = w_out[l].astype(BF16)
        xt = _mix_ffn_call(xt, od.reshape(t, DIFF_V_W), orr.reshape(t, RET_V_W),
                           wo[:DIFF_V_W], wo[DIFF_V_W:], row(norm_ffn2[l]),
                           w2_gate[l].astype(BF16), w2_up[l].astype(BF16),
                           w2_down[l].astype(BF16), row(norm_final), l == DEPTH - 1)
    return xt.reshape(b, s, d)
```

```python
import functools
import math

import numpy as np
import jax
import jax.numpy as jnp
from jax import lax
from jax.experimental import pallas as pl
from jax.experimental.pallas import tpu as pltpu

F32 = jnp.float32
BF16 = jnp.bfloat16

D_MODEL = 1024
DEPTH = 2
DIFF_HEADS = 4
DIFF_QK_DIM = 64
DIFF_V_DIM = 128
RET_HEADS = 4
RET_K_DIM = 64
RET_V_DIM = 128
D_FF = 2816
RET_CHUNK = 128
NORM_EPS = 1e-6
HALF_STEP = 0.5

DIFF_QK_W = DIFF_HEADS * 2 * DIFF_QK_DIM
DIFF_V_W = DIFF_HEADS * DIFF_V_DIM
RET_QK_W = RET_HEADS * RET_K_DIM
RET_V_W = RET_HEADS * RET_V_DIM
QKV_COLS = 2 * DIFF_QK_W + DIFF_V_W + 2 * RET_QK_W + RET_V_W
IN_COLS = QKV_COLS + RET_V_W

LANES = 128
MXU_COLS = 256

RQ_BLK = (2 * DIFF_QK_W + DIFF_V_W) // RET_QK_W
RK_BLK = RQ_BLK + 1
RV_BLK = (2 * DIFF_QK_W + DIFF_V_W + 2 * RET_QK_W) // RET_V_W

TOKEN_TILE = 512
FF_CHUNKS = ((0, 1536), (1536, 2816))
ATT_BLOCK = 512
ATT_HEADS_PER_STEP = 2
VMEM_LIMIT = 56 * 1024 * 1024

RET_LOG_DECAY = [float(v) for v in
                 np.log(1.0 - 2.0 ** (-5.0 - np.arange(RET_HEADS))).astype(np.float32)]
LOG2_E = math.log2(math.e)
DIFF_Q_SCALE = DIFF_QK_DIM ** -0.5 * LOG2_E
ALIBI_SLOPES_LOG2 = [float(2.0 ** (-8.0 * (h + 1) / DIFF_HEADS)) * LOG2_E
                     for h in range(DIFF_HEADS)]


def _rms(x, g):
    ms = jnp.mean(x * x, axis=-1, keepdims=True)
    return x * lax.rsqrt(ms + NORM_EPS) * g


def _dot(a, b):
    return jnp.dot(a, b, preferred_element_type=F32)


def _dot_nt(a, b):
    return lax.dot_general(a, b, (((1,), (1,)), ((), ())), preferred_element_type=F32)


def _dot_tn(a, b):
    return lax.dot_general(a, b, (((0,), (0,)), ((), ())), preferred_element_type=F32)


def _swiglu_residual(x, g_ref, wg_ref, wu_ref, wd_ref):
    xn = _rms(x, g_ref[...]).astype(BF16)
    y = None
    for c0, c1 in FF_CHUNKS:
        gate = _dot(xn, wg_ref[:, c0:c1])
        up = _dot(xn, wu_ref[:, c0:c1])
        h = (jax.nn.silu(gate) * up).astype(BF16)
        part = _dot(h, wd_ref[c0:c1, :])
        y = part if y is None else y + part
    return x + HALF_STEP * y


def _ffn_kernel(x_ref, g_ref, wg_ref, wu_ref, wd_ref, o_ref):
    o_ref[...] = _swiglu_residual(x_ref[...], g_ref, wg_ref, wu_ref, wd_ref)


def _mix_ffn_kernel(x_ref, od_ref, or_ref, woa_ref, wob_ref, g_ref, wg_ref, wu_ref, wd_ref,
                    gf_ref, o_ref, *, final_norm):
    x = x_ref[...] + _dot(od_ref[...], woa_ref[...]) + _dot(or_ref[...], wob_ref[...])
    x = _swiglu_residual(x, g_ref, wg_ref, wu_ref, wd_ref)
    if final_norm:
        x = _rms(x, gf_ref[...])
    o_ref[...] = x


def _inproj_kernel(x_ref, g_ref, w_ref, qkv_ref, rg_ref):
    xn = _rms(x_ref[...], g_ref[...]).astype(BF16)
    qkv_ref[:, :DIFF_QK_W] = (_dot(xn, w_ref[:, :DIFF_QK_W]) * DIFF_Q_SCALE).astype(BF16)
    qkv_ref[:, DIFF_QK_W:] = _dot(xn, w_ref[:, DIFF_QK_W:QKV_COLS]).astype(BF16)
    rg_ref[...] = _dot(xn, w_ref[:, QKV_COLS:])


def _resident(shape):
    return pl.BlockSpec(shape, lambda *_: (0,) * len(shape), pipeline_mode=pl.Buffered(1))


def _ffn_call(x, g, wg, wu, wd):
    t = x.shape[0]
    tile = pl.BlockSpec((TOKEN_TILE, D_MODEL), lambda i: (i, 0))
    return pl.pallas_call(
        _ffn_kernel,
        grid=(t // TOKEN_TILE,),
        in_specs=[tile, _resident((1, D_MODEL)), _resident((D_MODEL, D_FF)),
                  _resident((D_MODEL, D_FF)), _resident((D_FF, D_MODEL))],
        out_specs=tile,
        out_shape=jax.ShapeDtypeStruct((t, D_MODEL), F32),
        compiler_params=pltpu.CompilerParams(
            dimension_semantics=("arbitrary",), vmem_limit_bytes=VMEM_LIMIT),
        name="ffn",
    )(x, g, wg, wu, wd)


def _mix_ffn_call(x, od, orr, woa, wob, g, wg, wu, wd, gf, final_norm):
    t = x.shape[0]
    tile = pl.BlockSpec((TOKEN_TILE, D_MODEL), lambda i: (i, 0))
    half = pl.BlockSpec((TOKEN_TILE, DIFF_V_W), lambda i: (i, 0))
    return pl.pallas_call(
        functools.partial(_mix_ffn_kernel, final_norm=final_norm),
        grid=(t // TOKEN_TILE,),
        in_specs=[tile, half, half,
                  _resident((DIFF_V_W, D_MODEL)), _resident((RET_V_W, D_MODEL)),
                  _resident((1, D_MODEL)), _resident((D_MODEL, D_FF)),
                  _resident((D_MODEL, D_FF)), _resident((D_FF, D_MODEL)),
                  _resident((1, D_MODEL))],
        out_specs=tile,
        out_shape=jax.ShapeDtypeStruct((t, D_MODEL), F32),
        compiler_params=pltpu.CompilerParams(
            dimension_semantics=("arbitrary",), vmem_limit_bytes=VMEM_LIMIT),
        name="mix_ffn",
    )(x, od, orr, woa, wob, g, wg, wu, wd, gf)


def _inproj_call(x, g, w):
    t = x.shape[0]
    return pl.pallas_call(
        _inproj_kernel,
        grid=(t // TOKEN_TILE,),
        in_specs=[pl.BlockSpec((TOKEN_TILE, D_MODEL), lambda i: (i, 0)),
                  _resident((1, D_MODEL)), _resident((D_MODEL, IN_COLS))],
        out_specs=[pl.BlockSpec((TOKEN_TILE, QKV_COLS), lambda i: (i, 0)),
                   pl.BlockSpec((TOKEN_TILE, RET_V_W), lambda i: (i, 0))],
        out_shape=[jax.ShapeDtypeStruct((t, QKV_COLS), BF16),
                   jax.ShapeDtypeStruct((t, RET_V_W), F32)],
        compiler_params=pltpu.CompilerParams(
            dimension_semantics=("arbitrary",), vmem_limit_bytes=VMEM_LIMIT),
        name="inproj",
    )(x, g, w)


def _attn_kernel(lq1_ref, lk1_ref, lq2_ref, lk2_ref, sub_ref, q_ref, k_ref, v_ref, o_ref,
                 q2_s, m_s, l_s, acc_s, *, lam_init):
    bq = ATT_BLOCK
    hp = ATT_HEADS_PER_STEP
    g = pl.program_id(1)
    i = pl.program_id(2)
    lane = lax.broadcasted_iota(jnp.int32, (bq, LANES), 1)
    cols = lax.broadcasted_iota(jnp.int32, (1, bq), 1)

    slopes = []
    for hh in range(hp):
        h = g * hp + hh
        slope = jnp.float32(0.0)
        for hv in range(DIFF_HEADS):
            slope = jnp.where(h == hv, ALIBI_SLOPES_LOG2[hv], slope)
        slopes.append(slope.astype(F32))
        q = q_ref[0, :, hh * LANES:(hh + 1) * LANES]
        zero = jnp.zeros_like(q)
        q2_s[hh, 0:bq, :] = jnp.where(lane < DIFF_QK_DIM, q, zero)
        q2_s[hh, bq:2 * bq, :] = jnp.where(lane >= DIFF_QK_DIM, q, zero)

    def scores(hh, j):
        off = pl.multiple_of(j * bq, bq)
        k = k_ref[0, pl.ds(off, bq), hh * LANES:(hh + 1) * LANES]
        v = v_ref[0, pl.ds(off, bq), hh * LANES:(hh + 1) * LANES]
        s = _dot_nt(q2_s[hh], k)
        bias = slopes[hh] * (cols + (j - i) * bq).astype(F32)
        return s + bias, v

    def split(s):
        return [s[:, c * LANES:(c + 1) * LANES] for c in range(bq // LANES)]

    def total(ps):
        return functools.reduce(lambda a, b: a + b, ps)

    rows = lax.broadcasted_iota(jnp.int32, (2 * bq, bq), 0)
    rows = jnp.where(rows >= bq, rows - bq, rows)
    visible = rows >= lax.broadcasted_iota(jnp.int32, (2 * bq, bq), 1)
    for hh in range(hp):
        s, v = scores(hh, i)
        s = jnp.where(visible, s, -jnp.inf)
        m0 = jnp.broadcast_to(jnp.max(s, axis=-1, keepdims=True), (2 * bq, LANES))
        ps = [jnp.exp2(c - m0) for c in split(s)]
        m_s[hh] = m0
        l_s[hh] = total(ps)
        acc_s[hh] = _dot(jnp.concatenate(ps, axis=1).astype(BF16), v)

    def body(j, carry):
        for hh in range(hp):
            s, v = scores(hh, j)
            m_prev = m_s[hh]
            m_new = jnp.maximum(m_prev, jnp.max(s, axis=-1, keepdims=True))
            alpha = jnp.exp2(m_prev - m_new)
            ps = [jnp.exp2(c - m_new) for c in split(s)]
            l_s[hh] = alpha * l_s[hh] + total(ps)
            acc_s[hh] = alpha * acc_s[hh] + _dot(jnp.concatenate(ps, axis=1).astype(BF16), v)
            m_s[hh] = m_new
        return carry

    lax.fori_loop(0, i, body, 0)

    lam = (jnp.exp(jnp.sum(lq1_ref[...] * lk1_ref[...], axis=-1, keepdims=True))
           - jnp.exp(jnp.sum(lq2_ref[...] * lk2_ref[...], axis=-1, keepdims=True))
           + lam_init)
    for hh in range(hp):
        l = jnp.sum(l_s[hh], axis=-1, keepdims=True)
        on = acc_s[hh] / l
        o = on[0:bq, :] - lam * on[bq:2 * bq, :]
        o = o * lax.rsqrt(jnp.mean(o * o, axis=-1, keepdims=True) + NORM_EPS)
        o_ref[0, :, hh * LANES:(hh + 1) * LANES] = (
            o * sub_ref[...] * (1.0 - lam_init)).astype(BF16)


def _attn_call(qkv, lq1, lk1, lq2, lk2, sub, lam_init):
    b, s, _ = qkv.shape
    bq = ATT_BLOCK
    hp = ATT_HEADS_PER_STEP
    w = hp * LANES
    vec = lambda n: pl.BlockSpec((1, n), lambda bi, gi, qi: (0, 0))
    return pl.pallas_call(
        functools.partial(_attn_kernel, lam_init=lam_init),
        grid=(b, DIFF_HEADS // hp, s // bq),
        in_specs=[vec(DIFF_QK_DIM), vec(DIFF_QK_DIM), vec(DIFF_QK_DIM), vec(DIFF_QK_DIM),
                  vec(DIFF_V_DIM),
                  pl.BlockSpec((1, bq, w), lambda bi, gi, qi: (bi, qi, gi)),
                  pl.BlockSpec((1, s, w), lambda bi, gi, qi: (bi, 0, DIFF_QK_W // w + gi)),
                  pl.BlockSpec((1, s, w), lambda bi, gi, qi: (bi, 0, 2 * DIFF_QK_W // w + gi))],
        out_specs=pl.BlockSpec((1, bq, w), lambda bi, gi, qi: (bi, qi, gi)),
        out_shape=jax.ShapeDtypeStruct((b, s, DIFF_V_W), BF16),
        scratch_shapes=[pltpu.VMEM((hp, 2 * bq, LANES), BF16),
                        pltpu.VMEM((hp, 2 * bq, LANES), F32),
                        pltpu.VMEM((hp, 2 * bq, LANES), F32),
                        pltpu.VMEM((hp, 2 * bq, DIFF_V_DIM), F32)],
        compiler_params=pltpu.CompilerParams(
            dimension_semantics=("arbitrary", "arbitrary", "arbitrary"),
            vmem_limit_bytes=VMEM_LIMIT),
        name="diff_attn",
    )(lq1, lk1, lq2, lk2, sub, qkv, qkv, qkv)


def _ret_kernel(q_ref, k_ref, v_ref, g_ref, o_ref, st_s, dec_s):
    c = RET_CHUNK
    n_chunks = q_ref.shape[1] // c
    r = lax.broadcasted_iota(jnp.int32, (c, c), 0)
    cc = lax.broadcasted_iota(jnp.int32, (c, c), 1)
    rel = (r - cc).astype(F32)
    rf = r.astype(F32)
    for h in range(RET_HEADS):
        lg = RET_LOG_DECAY[h]
        dec_s[h, 0] = jnp.where(rel >= 0, jnp.exp(lg * rel), 0.0)
        dec_s[h, 1] = jnp.exp(lg * (c - 1.0 - rf))
        dec_s[h, 2] = jnp.exp(lg * (rf + 1.0))
        st_s[h] = jnp.zeros((c, c), F32)
    upper = cc >= RET_K_DIM

    def body(n, carry):
        off = pl.multiple_of(n * c, c)
        for p in range(RET_HEADS // 2):
            qp = q_ref[0, pl.ds(off, c), p * LANES:(p + 1) * LANES].astype(F32)
            kp = k_ref[0, pl.ds(off, c), p * LANES:(p + 1) * LANES].astype(F32) * (RET_K_DIM ** -0.5)
            for e in range(2):
                h = 2 * p + e
                sel = upper if e else jnp.logical_not(upper)
                qh = jnp.where(sel, qp, 0.0)
                kh = jnp.where(sel, kp, 0.0)
                v = v_ref[0, pl.ds(off, c), h * RET_V_DIM:(h + 1) * RET_V_DIM]
                sc = _dot_nt(qh.astype(BF16), kh.astype(BF16)) * dec_s[h, 0]
                state = st_s[h]
                o = (_dot(sc.astype(BF16), v)
                     + _dot((qh * dec_s[h, 2]).astype(BF16), state.astype(BF16)))
                o = o * lax.rsqrt(jnp.mean(o * o, axis=-1, keepdims=True) + NORM_EPS)
                gate = g_ref[0, pl.ds(off, c), h * RET_V_DIM:(h + 1) * RET_V_DIM]
                o_ref[0, pl.ds(off, c), h * RET_V_DIM:(h + 1) * RET_V_DIM] = (
                    jax.nn.silu(gate) * o).astype(BF16)
                kv = _dot_tn((kh * dec_s[h, 1]).astype(BF16), v)
                st_s[h] = state * math.exp(RET_LOG_DECAY[h] * c) + kv
        return carry

    lax.fori_loop(0, n_chunks, body, 0)


def _ret_call(qkv, rg):
    b, s, _ = qkv.shape
    return pl.pallas_call(
        _ret_kernel,
        grid=(b,),
        in_specs=[pl.BlockSpec((1, s, RET_QK_W), lambda bi: (bi, 0, RQ_BLK)),
                  pl.BlockSpec((1, s, RET_QK_W), lambda bi: (bi, 0, RK_BLK)),
                  pl.BlockSpec((1, s, RET_V_W), lambda bi: (bi, 0, RV_BLK)),
                  pl.BlockSpec((1, s, RET_V_W), lambda bi: (bi, 0, 0))],
        out_specs=pl.BlockSpec((1, s, RET_V_W), lambda bi: (bi, 0, 0)),
        out_shape=jax.ShapeDtypeStruct((b, s, RET_V_W), BF16),
        scratch_shapes=[pltpu.VMEM((RET_HEADS, RET_CHUNK, RET_CHUNK), F32),
                        pltpu.VMEM((RET_HEADS, 3, RET_CHUNK, RET_CHUNK), F32)],
        compiler_params=pltpu.CompilerParams(
            dimension_semantics=("arbitrary",), vmem_limit_bytes=VMEM_LIMIT),
        name="retention",
    )(qkv, qkv, qkv, rg)


def kernel(x, norm_ffn1, w1_gate, w1_up, w1_down, norm_mix, w_in, lambda_q1, lambda_k1, lambda_q2, lambda_k2, diff_subln, w_out, norm_ffn2, w2_gate, w2_up, w2_down, norm_final):
    b, s, d = x.shape
    t = b * s
    xt = x.reshape(t, d)
    row = lambda a: a.reshape(1, -1).astype(F32)
    for l in range(DEPTH):
        lam_init = 0.8 - 0.6 * math.exp(-0.3 * l)
        xt = _ffn_call(xt, row(norm_ffn1[l]), w1_gate[l].astype(BF16), w1_up[l].astype(BF16),
                       w1_down[l].astype(BF16))
        qkv, rg = _inproj_call(xt, row(norm_mix[l]), w_in[l].astype(BF16))
        qkv = qkv.reshape(b, s, QKV_COLS)
        od = _attn_call(qkv, row(lambda_q1[l]), row(lambda_k1[l]), row(lambda_q2[l]),
                        row(lambda_k2[l]), row(diff_subln[l]), lam_init)
        orr = _ret_call(qkv, rg.reshape(b, s, RET_V_W))
        wo = w_out[l].astype(BF16)
        xt = _mix_ffn_call(xt, od.reshape(t, DIFF_V_W), orr.reshape(t, RET_V_W),
                           wo[:DIFF_V_W], wo[DIFF_V_W:], row(norm_ffn2[l]),
                           w2_gate[l].astype(BF16), w2_up[l].astype(BF16),
                           w2_down[l].astype(BF16), row(norm_final), l == DEPTH - 1)
    return xt.reshape(b, s, d)
```

```python
import functools
import math

import numpy as np
import jax
import jax.numpy as jnp
from jax import lax
from jax.experimental import pallas as pl
from jax.experimental.pallas import tpu as pltpu

F32 = jnp.float32
BF16 = jnp.bfloat16

D_MODEL = 1024
DEPTH = 2
DIFF_HEADS = 4
DIFF_QK_DIM = 64
DIFF_V_DIM = 128
RET_HEADS = 4
RET_K_DIM = 64
RET_V_DIM = 128
D_FF = 2816
RET_CHUNK = 128
NORM_EPS = 1e-6
HALF_STEP = 0.5

DIFF_QK_W = DIFF_HEADS * 2 * DIFF_QK_DIM
DIFF_V_W = DIFF_HEADS * DIFF_V_DIM
RET_QK_W = RET_HEADS * RET_K_DIM
RET_V_W = RET_HEADS * RET_V_DIM
QKV_COLS = 2 * DIFF_QK_W + DIFF_V_W + 2 * RET_QK_W + RET_V_W
IN_COLS = QKV_COLS + RET_V_W

LANES = 128
MXU_COLS = 256

RQ_BLK = (2 * DIFF_QK_W + DIFF_V_W) // RET_QK_W
RK_BLK = RQ_BLK + 1
RV_BLK = (2 * DIFF_QK_W + DIFF_V_W + 2 * RET_QK_W) // RET_V_W

TOKEN_TILE = 1024
FF_CHUNKS = ((0, 1024), (1024, 2048), (2048, 2816))
ATT_BLOCK = 512
ATT_HEADS_PER_STEP = 4
RET_UNROLL = 4
VMEM_LIMIT = 56 * 1024 * 1024

RET_LOG_DECAY = [float(v) for v in
                 np.log(1.0 - 2.0 ** (-5.0 - np.arange(RET_HEADS))).astype(np.float32)]
LOG2_E = math.log2(math.e)
DIFF_Q_SCALE = DIFF_QK_DIM ** -0.5 * LOG2_E
ALIBI_SLOPES_LOG2 = [float(2.0 ** (-8.0 * (h + 1) / DIFF_HEADS)) * LOG2_E
                     for h in range(DIFF_HEADS)]


def _rms(x, g):
    ms = jnp.mean(x * x, axis=-1, keepdims=True)
    return x * lax.rsqrt(ms + NORM_EPS) * g


def _dot(a, b):
    return jnp.dot(a, b, preferred_element_type=F32)


def _dot_nt(a, b):
    return lax.dot_general(a, b, (((1,), (1,)), ((), ())), preferred_element_type=F32)


def _dot_tn(a, b):
    return lax.dot_general(a, b, (((0,), (0,)), ((), ())), preferred_element_type=F32)


def _swiglu_residual(x, g_ref, wg_ref, wu_ref, wd_ref):
    xn = _rms(x, g_ref[...]).astype(BF16)
    y = None
    for c0, c1 in FF_CHUNKS:
        gate = _dot(xn, wg_ref[:, c0:c1])
        up = _dot(xn, wu_ref[:, c0:c1])
        h = (jax.nn.silu(gate) * up).astype(BF16)
        part = _dot(h, wd_ref[c0:c1, :])
        y = part if y is None else y + part
    return x + HALF_STEP * y


def _ffn_kernel(x_ref, g_ref, wg_ref, wu_ref, wd_ref, o_ref):
    o_ref[...] = _swiglu_residual(x_ref[...], g_ref, wg_ref, wu_ref, wd_ref)


def _mix_ffn_kernel(x_ref, od_ref, or_ref, woa_ref, wob_ref, g_ref, wg_ref, wu_ref, wd_ref,
                    gf_ref, o_ref, *, final_norm):
    x = x_ref[...] + _dot(od_ref[...], woa_ref[...]) + _dot(or_ref[...], wob_ref[...])
    x = _swiglu_residual(x, g_ref, wg_ref, wu_ref, wd_ref)
    if final_norm:
        x = _rms(x, gf_ref[...])
    o_ref[...] = x


def _inproj_kernel(x_ref, g_ref, w_ref, qkv_ref, rg_ref):
    xn = _rms(x_ref[...], g_ref[...]).astype(BF16)
    qkv_ref[:, :DIFF_QK_W] = (_dot(xn, w_ref[:, :DIFF_QK_W]) * DIFF_Q_SCALE).astype(BF16)
    qkv_ref[:, DIFF_QK_W:] = _dot(xn, w_ref[:, DIFF_QK_W:QKV_COLS]).astype(BF16)
    rg_ref[...] = _dot(xn, w_ref[:, QKV_COLS:])


def _resident(shape):
    return pl.BlockSpec(shape, lambda *_: (0,) * len(shape), pipeline_mode=pl.Buffered(1))


def _ffn_call(x, g, wg, wu, wd):
    t = x.shape[0]
    tile = pl.BlockSpec((TOKEN_TILE, D_MODEL), lambda i: (i, 0))
    return pl.pallas_call(
        _ffn_kernel,
        grid=(t // TOKEN_TILE,),
        in_specs=[tile, _resident((1, D_MODEL)), _resident((D_MODEL, D_FF)),
                  _resident((D_MODEL, D_FF)), _resident((D_FF, D_MODEL))],
        out_specs=tile,
        out_shape=jax.ShapeDtypeStruct((t, D_MODEL), F32),
        compiler_params=pltpu.CompilerParams(
            dimension_semantics=("arbitrary",), vmem_limit_bytes=VMEM_LIMIT),
        name="ffn",
    )(x, g, wg, wu, wd)


def _mix_ffn_call(x, od, orr, woa, wob, g, wg, wu, wd, gf, final_norm):
    t = x.shape[0]
    tile = pl.BlockSpec((TOKEN_TILE, D_MODEL), lambda i: (i, 0))
    half = pl.BlockSpec((TOKEN_TILE, DIFF_V_W), lambda i: (i, 0))
    return pl.pallas_call(
        functools.partial(_mix_ffn_kernel, final_norm=final_norm),
        grid=(t // TOKEN_TILE,),
        in_specs=[tile, half, half,
                  _resident((DIFF_V_W, D_MODEL)), _resident((RET_V_W, D_MODEL)),
                  _resident((1, D_MODEL)), _resident((D_MODEL, D_FF)),
                  _resident((D_MODEL, D_FF)), _resident((D_FF, D_MODEL)),
                  _resident((1, D_MODEL))],
        out_specs=tile,
        out_shape=jax.ShapeDtypeStruct((t, D_MODEL), F32),
        compiler_params=pltpu.CompilerParams(
            dimension_semantics=("arbitrary",), vmem_limit_bytes=VMEM_LIMIT),
        name="mix_ffn",
    )(x, od, orr, woa, wob, g, wg, wu, wd, gf)


def _inproj_call(x, g, w):
    t = x.shape[0]
    return pl.pallas_call(
        _inproj_kernel,
        grid=(t // TOKEN_TILE,),
        in_specs=[pl.BlockSpec((TOKEN_TILE, D_MODEL), lambda i: (i, 0)),
                  _resident((1, D_MODEL)), _resident((D_MODEL, IN_COLS))],
        out_specs=[pl.BlockSpec((TOKEN_TILE, QKV_COLS), lambda i: (i, 0)),
                   pl.BlockSpec((TOKEN_TILE, RET_V_W), lambda i: (i, 0))],
        out_shape=[jax.ShapeDtypeStruct((t, QKV_COLS), BF16),
                   jax.ShapeDtypeStruct((t, RET_V_W), F32)],
        compiler_params=pltpu.CompilerParams(
            dimension_semantics=("arbitrary",), vmem_limit_bytes=VMEM_LIMIT),
        name="inproj",
    )(x, g, w)


def _attn_kernel(lq1_ref, lk1_ref, lq2_ref, lk2_ref, sub_ref, q_ref, k_ref, v_ref, o_ref,
                 q2_s, m_s, l_s, acc_s, *, lam_init):
    bq = ATT_BLOCK
    hp = ATT_HEADS_PER_STEP
    g = pl.program_id(1)
    i = pl.program_id(2)
    lane = lax.broadcasted_iota(jnp.int32, (bq, LANES), 1)
    cols = lax.broadcasted_iota(jnp.int32, (1, bq), 1)

    slopes = []
    for hh in range(hp):
        h = g * hp + hh
        slope = jnp.float32(0.0)
        for hv in range(DIFF_HEADS):
            slope = jnp.where(h == hv, ALIBI_SLOPES_LOG2[hv], slope)
        slopes.append(slope.astype(F32))
        q = q_ref[0, :, hh * LANES:(hh + 1) * LANES]
        zero = jnp.zeros_like(q)
        q2_s[hh, 0:bq, :] = jnp.where(lane < DIFF_QK_DIM, q, zero)
        q2_s[hh, bq:2 * bq, :] = jnp.where(lane >= DIFF_QK_DIM, q, zero)

    def scores(hh, j):
        off = pl.multiple_of(j * bq, bq)
        k = k_ref[0, pl.ds(off, bq), hh * LANES:(hh + 1) * LANES]
        v = v_ref[0, pl.ds(off, bq), hh * LANES:(hh + 1) * LANES]
        s = _dot_nt(q2_s[hh], k)
        bias = slopes[hh] * (cols + (j - i) * bq).astype(F32)
        return s + bias, v

    def split(s):
        return [s[:, c * LANES:(c + 1) * LANES] for c in range(bq // LANES)]

    def total(ps):
        return functools.reduce(lambda a, b: a + b, ps)

    rows = lax.broadcasted_iota(jnp.int32, (2 * bq, bq), 0)
    rows = jnp.where(rows >= bq, rows - bq, rows)
    visible = rows >= lax.broadcasted_iota(jnp.int32, (2 * bq, bq), 1)
    svs = [scores(hh, i) for hh in range(hp)]
    for hh in range(hp):
        s, v = svs[hh]
        s = jnp.where(visible, s, -jnp.inf)
        m0 = jnp.broadcast_to(jnp.max(s, axis=-1, keepdims=True), (2 * bq, LANES))
        ps = [jnp.exp2(c - m0) for c in split(s)]
        m_s[hh] = m0
        l_s[hh] = total(ps)
        acc_s[hh] = _dot(jnp.concatenate(ps, axis=1).astype(BF16), v)

    def body(j, carry):
        svs = [scores(hh, j) for hh in range(hp)]
        for hh in range(hp):
            s, v = svs[hh]
            m_prev = m_s[hh]
            m_new = jnp.maximum(m_prev, jnp.max(s, axis=-1, keepdims=True))
            alpha = jnp.exp2(m_prev - m_new)
            ps = [jnp.exp2(c - m_new) for c in split(s)]
            l_s[hh] = alpha * l_s[hh] + total(ps)
            acc_s[hh] = alpha * acc_s[hh] + _dot(jnp.concatenate(ps, axis=1).astype(BF16), v)
            m_s[hh] = m_new
        return carry

    lax.fori_loop(0, i, body, 0)

    lam = (jnp.exp(jnp.sum(lq1_ref[...] * lk1_ref[...], axis=-1, keepdims=True))
           - jnp.exp(jnp.sum(lq2_ref[...] * lk2_ref[...], axis=-1, keepdims=True))
           + lam_init)
    for hh in range(hp):
        l = jnp.sum(l_s[hh], axis=-1, keepdims=True)
        on = acc_s[hh] / l
        o = on[0:bq, :] - lam * on[bq:2 * bq, :]
        o = o * lax.rsqrt(jnp.mean(o * o, axis=-1, keepdims=True) + NORM_EPS)
        o_ref[0, :, hh * LANES:(hh + 1) * LANES] = (
            o * sub_ref[...] * (1.0 - lam_init)).astype(BF16)


def _attn_call(qkv, lq1, lk1, lq2, lk2, sub, lam_init):
    b, s, _ = qkv.shape
    bq = ATT_BLOCK
    hp = ATT_HEADS_PER_STEP
    w = hp * LANES
    vec = lambda n: pl.BlockSpec((1, n), lambda bi, gi, qi: (0, 0))
    return pl.pallas_call(
        functools.partial(_attn_kernel, lam_init=lam_init),
        grid=(b, DIFF_HEADS // hp, s // bq),
        in_specs=[vec(DIFF_QK_DIM), vec(DIFF_QK_DIM), vec(DIFF_QK_DIM), vec(DIFF_QK_DIM),
                  vec(DIFF_V_DIM),
                  pl.BlockSpec((1, bq, w), lambda bi, gi, qi: (bi, qi, gi)),
                  pl.BlockSpec((1, s, w), lambda bi, gi, qi: (bi, 0, DIFF_QK_W // w + gi)),
                  pl.BlockSpec((1, s, w), lambda bi, gi, qi: (bi, 0, 2 * DIFF_QK_W // w + gi))],
        out_specs=pl.BlockSpec((1, bq, w), lambda bi, gi, qi: (bi, qi, gi)),
        out_shape=jax.ShapeDtypeStruct((b, s, DIFF_V_W), BF16),
        scratch_shapes=[pltpu.VMEM((hp, 2 * bq, LANES), BF16),
                        pltpu.VMEM((hp, 2 * bq, LANES), F32),
                        pltpu.VMEM((hp, 2 * bq, LANES), F32),
                        pltpu.VMEM((hp, 2 * bq, DIFF_V_DIM), F32)],
        compiler_params=pltpu.CompilerParams(
            dimension_semantics=("arbitrary", "arbitrary", "arbitrary"),
            vmem_limit_bytes=VMEM_LIMIT),
        name="diff_attn",
    )(lq1, lk1, lq2, lk2, sub, qkv, qkv, qkv)


def _ret_kernel(q_ref, k_ref, v_ref, g_ref, o_ref, kv_s, prev_s, dec_s):
    c = RET_CHUNK
    n_chunks = q_ref.shape[1] // c
    r = lax.broadcasted_iota(jnp.int32, (c, c), 0)
    cc = lax.broadcasted_iota(jnp.int32, (c, c), 1)
    rel = (r - cc).astype(F32)
    rf = r.astype(F32)
    for h in range(RET_HEADS):
        lg = RET_LOG_DECAY[h]
        dec_s[h, 0] = jnp.where(rel >= 0, jnp.exp(lg * rel), 0.0)
        dec_s[h, 1] = jnp.exp(lg * (c - 1.0 - rf))
        dec_s[h, 2] = jnp.exp(lg * (rf + 1.0))
    upper = cc >= RET_K_DIM

    def head_slab(ref, off, p, e):
        slab = ref[0, pl.ds(off, c), p * LANES:(p + 1) * LANES].astype(F32)
        return jnp.where(upper if e else jnp.logical_not(upper), slab, 0.0)

    def kv_body(n, carry):
        off = pl.multiple_of(n * c, c)
        for h in range(RET_HEADS):
            kh = head_slab(k_ref, off, h // 2, h % 2) * (RET_K_DIM ** -0.5)
            v = v_ref[0, pl.ds(off, c), h * RET_V_DIM:(h + 1) * RET_V_DIM]
            kv_s[h, n] = _dot_tn((kh * dec_s[h, 1]).astype(BF16), v)
        return carry

    lax.fori_loop(0, n_chunks, kv_body, 0, unroll=RET_UNROLL)

    for h in range(RET_HEADS):
        state = jnp.zeros((c, RET_V_DIM), F32)
        for n in range(n_chunks):
            prev_s[h, n] = state.astype(BF16)
            state = state * math.exp(RET_LOG_DECAY[h] * c) + kv_s[h, n]

    def out_body(n, carry):
        off = pl.multiple_of(n * c, c)
        qhs = [head_slab(q_ref, off, h // 2, h % 2) for h in range(RET_HEADS)]
        scs = [_dot_nt(qhs[h].astype(BF16),
                       (head_slab(k_ref, off, h // 2, h % 2) * (RET_K_DIM ** -0.5)).astype(BF16))
               for h in range(RET_HEADS)]
        for h in range(RET_HEADS):
            qh = qhs[h]
            v = v_ref[0, pl.ds(off, c), h * RET_V_DIM:(h + 1) * RET_V_DIM]
            sc = scs[h] * dec_s[h, 0]
            lhs = jnp.concatenate([sc.astype(BF16), (qh * dec_s[h, 2]).astype(BF16)], axis=1)
            rhs = jnp.concatenate([v, prev_s[h, n]], axis=0)
            o = _dot(lhs, rhs)
            o = o * lax.rsqrt(jnp.mean(o * o, axis=-1, keepdims=True) + NORM_EPS)
            gate = g_ref[0, pl.ds(off, c), h * RET_V_DIM:(h + 1) * RET_V_DIM]
            o_ref[0, pl.ds(off, c), h * RET_V_DIM:(h + 1) * RET_V_DIM] = (
                jax.nn.silu(gate) * o).astype(BF16)
        return carry

    lax.fori_loop(0, n_chunks, out_body, 0, unroll=RET_UNROLL)


def _ret_call(qkv, rg):
    b, s, _ = qkv.shape
    return pl.pallas_call(
        _ret_kernel,
        grid=(b,),
        in_specs=[pl.BlockSpec((1, s, RET_QK_W), lambda bi: (bi, 0, RQ_BLK)),
                  pl.BlockSpec((1, s, RET_QK_W), lambda bi: (bi, 0, RK_BLK)),
                  pl.BlockSpec((1, s, RET_V_W), lambda bi: (bi, 0, RV_BLK)),
                  pl.BlockSpec((1, s, RET_V_W), lambda bi: (bi, 0, 0))],
        out_specs=pl.BlockSpec((1, s, RET_V_W), lambda bi: (bi, 0, 0)),
        out_shape=jax.ShapeDtypeStruct((b, s, RET_V_W), BF16),
        scratch_shapes=[pltpu.VMEM((RET_HEADS, s // RET_CHUNK, LANES, RET_V_DIM), F32),
                        pltpu.VMEM((RET_HEADS, s // RET_CHUNK, LANES, RET_V_DIM), BF16),
                        pltpu.VMEM((RET_HEADS, 3, RET_CHUNK, RET_CHUNK), F32)],
        compiler_params=pltpu.CompilerParams(
            dimension_semantics=("arbitrary",), vmem_limit_bytes=VMEM_LIMIT),
        name="retention",
    )(qkv, qkv, qkv, rg)


def kernel(x, norm_ffn1, w1_gate, w1_up, w1_down, norm_mix, w_in, lambda_q1, lambda_k1, lambda_q2, lambda_k2, diff_subln, w_out, norm_ffn2, w2_gate, w2_up, w2_down, norm_final):
    b, s, d = x.shape
    t = b * s
    xt = x.reshape(t, d)
    row = lambda a: a.reshape(1, -1).astype(F32)
    for l in range(DEPTH):
        lam_init = 0.8 - 0.6 * math.exp(-0.3 * l)
        xt = _ffn_call(xt, row(norm_ffn1[l]), w1_gate[l].astype(BF16), w1_up[l].astype(BF16),
                       w1_down[l].astype(BF16))
        qkv, rg = _inproj_call(xt, row(norm_mix[l]), w_in[l].astype(BF16))
        qkv = qkv.reshape(b, s, QKV_COLS)
        od = _attn_call(qkv, row(lambda_q1[l]), row(lambda_k1[l]), row(lambda_q2[l]),
                        row(lambda_k2[l]), row(diff_subln[l]), lam_init)
        orr = _ret_call(qkv, rg.reshape(b, s, RET_V_W))
        wo = w_out[l].astype(BF16)
        xt = _mix_ffn_call(xt, od.reshape(t, DIFF_V_W), orr.reshape(t, RET_V_W),
                           wo[:DIFF_V_W], wo[DIFF_V_W:], row(norm_ffn2[l]),
                           w2_gate[l].astype(BF16), w2_up[l].astype(BF16),
                           w2_down[l].astype(BF16), row(norm_final), l == DEPTH - 1)
    return xt.reshape(b, s, d)
```

```python
import functools
import math

import numpy as np
import jax
import jax.numpy as jnp
from jax import lax
from jax.experimental import pallas as pl
from jax.experimental.pallas import tpu as pltpu

F32 = jnp.float32
BF16 = jnp.bfloat16

D_MODEL = 1024
DEPTH = 2
DIFF_HEADS = 4
DIFF_QK_DIM = 64
DIFF_V_DIM = 128
RET_HEADS = 4
RET_K_DIM = 64
RET_V_DIM = 128
D_FF = 2816
RET_CHUNK = 128
NORM_EPS = 1e-6
HALF_STEP = 0.5

DIFF_QK_W = DIFF_HEADS * 2 * DIFF_QK_DIM
DIFF_V_W = DIFF_HEADS * DIFF_V_DIM
RET_QK_W = RET_HEADS * RET_K_DIM
RET_V_W = RET_HEADS * RET_V_DIM
QKV_COLS = 2 * DIFF_QK_W + DIFF_V_W + 2 * RET_QK_W + RET_V_W
IN_COLS = QKV_COLS + RET_V_W

LANES = 128
MXU_COLS = 256

RQ_BLK = (2 * DIFF_QK_W + DIFF_V_W) // RET_QK_W
RK_BLK = RQ_BLK + 1
RV_BLK = (2 * DIFF_QK_W + DIFF_V_W + 2 * RET_QK_W) // RET_V_W

TOKEN_TILE = 1024
FF_CHUNKS = ((0, 1024), (1024, 2048), (2048, 2816))
ATT_BLOCK = 512
ATT_HEADS_PER_STEP = 4
ATT_SCORE_LOOKAHEAD = 4
RET_UNROLL = 4
VMEM_LIMIT = 56 * 1024 * 1024

RET_LOG_DECAY = [float(v) for v in
                 np.log(1.0 - 2.0 ** (-5.0 - np.arange(RET_HEADS))).astype(np.float32)]
LOG2_E = math.log2(math.e)
DIFF_Q_SCALE = DIFF_QK_DIM ** -0.5 * LOG2_E
ALIBI_SLOPES_LOG2 = [float(2.0 ** (-8.0 * (h + 1) / DIFF_HEADS)) * LOG2_E
                     for h in range(DIFF_HEADS)]


def _rms(x, g):
    ms = jnp.mean(x * x, axis=-1, keepdims=True)
    return x * lax.rsqrt(ms + NORM_EPS) * g


def _dot(a, b):
    return jnp.dot(a, b, preferred_element_type=F32)


def _dot_nt(a, b):
    return lax.dot_general(a, b, (((1,), (1,)), ((), ())), preferred_element_type=F32)


def _dot_tn(a, b):
    return lax.dot_general(a, b, (((0,), (0,)), ((), ())), preferred_element_type=F32)


def _swiglu_residual(x, g_ref, wg_ref, wu_ref, wd_ref):
    xn = _rms(x, g_ref[...]).astype(BF16)
    y = None
    for c0, c1 in FF_CHUNKS:
        gate = _dot(xn, wg_ref[:, c0:c1])
        up = _dot(xn, wu_ref[:, c0:c1])
        h = (jax.nn.silu(gate) * up).astype(BF16)
        part = _dot(h, wd_ref[c0:c1, :])
        y = part if y is None else y + part
    return x + HALF_STEP * y


def _ffn_kernel(x_ref, g_ref, wg_ref, wu_ref, wd_ref, o_ref):
    o_ref[...] = _swiglu_residual(x_ref[...], g_ref, wg_ref, wu_ref, wd_ref)


def _mix_ffn_kernel(x_ref, od_ref, or_ref, woa_ref, wob_ref, g_ref, wg_ref, wu_ref, wd_ref,
                    gf_ref, o_ref, *, final_norm):
    x = x_ref[...] + _dot(od_ref[...], woa_ref[...]) + _dot(or_ref[...], wob_ref[...])
    x = _swiglu_residual(x, g_ref, wg_ref, wu_ref, wd_ref)
    if final_norm:
        x = _rms(x, gf_ref[...])
    o_ref[...] = x


def _inproj_kernel(x_ref, g_ref, w_ref, qkv_ref, rg_ref):
    xn = _rms(x_ref[...], g_ref[...]).astype(BF16)
    qkv_ref[:, :DIFF_QK_W] = (_dot(xn, w_ref[:, :DIFF_QK_W]) * DIFF_Q_SCALE).astype(BF16)
    qkv_ref[:, DIFF_QK_W:] = _dot(xn, w_ref[:, DIFF_QK_W:QKV_COLS]).astype(BF16)
    rg_ref[...] = _dot(xn, w_ref[:, QKV_COLS:])


def _resident(shape):
    return pl.BlockSpec(shape, lambda *_: (0,) * len(shape), pipeline_mode=pl.Buffered(1))


def _ffn_call(x, g, wg, wu, wd):
    t = x.shape[0]
    tile = pl.BlockSpec((TOKEN_TILE, D_MODEL), lambda i: (i, 0))
    return pl.pallas_call(
        _ffn_kernel,
        grid=(t // TOKEN_TILE,),
        in_specs=[tile, _resident((1, D_MODEL)), _resident((D_MODEL, D_FF)),
                  _resident((D_MODEL, D_FF)), _resident((D_FF, D_MODEL))],
        out_specs=tile,
        out_shape=jax.ShapeDtypeStruct((t, D_MODEL), F32),
        compiler_params=pltpu.CompilerParams(
            dimension_semantics=("arbitrary",), vmem_limit_bytes=VMEM_LIMIT),
        name="ffn",
    )(x, g, wg, wu, wd)


def _mix_ffn_call(x, od, orr, woa, wob, g, wg, wu, wd, gf, final_norm):
    t = x.shape[0]
    tile = pl.BlockSpec((TOKEN_TILE, D_MODEL), lambda i: (i, 0))
    half = pl.BlockSpec((TOKEN_TILE, DIFF_V_W), lambda i: (i, 0))
    return pl.pallas_call(
        functools.partial(_mix_ffn_kernel, final_norm=final_norm),
        grid=(t // TOKEN_TILE,),
        in_specs=[tile, half, half,
                  _resident((DIFF_V_W, D_MODEL)), _resident((RET_V_W, D_MODEL)),
                  _resident((1, D_MODEL)), _resident((D_MODEL, D_FF)),
                  _resident((D_MODEL, D_FF)), _resident((D_FF, D_MODEL)),
                  _resident((1, D_MODEL))],
        out_specs=tile,
        out_shape=jax.ShapeDtypeStruct((t, D_MODEL), F32),
        compiler_params=pltpu.CompilerParams(
            dimension_semantics=("arbitrary",), vmem_limit_bytes=VMEM_LIMIT),
        name="mix_ffn",
    )(x, od, orr, woa, wob, g, wg, wu, wd, gf)


def _inproj_call(x, g, w):
    t = x.shape[0]
    return pl.pallas_call(
        _inproj_kernel,
        grid=(t // TOKEN_TILE,),
        in_specs=[pl.BlockSpec((TOKEN_TILE, D_MODEL), lambda i: (i, 0)),
                  _resident((1, D_MODEL)), _resident((D_MODEL, IN_COLS))],
        out_specs=[pl.BlockSpec((TOKEN_TILE, QKV_COLS), lambda i: (i, 0)),
                   pl.BlockSpec((TOKEN_TILE, RET_V_W), lambda i: (i, 0))],
        out_shape=[jax.ShapeDtypeStruct((t, QKV_COLS), BF16),
                   jax.ShapeDtypeStruct((t, RET_V_W), F32)],
        compiler_params=pltpu.CompilerParams(
            dimension_semantics=("arbitrary",), vmem_limit_bytes=VMEM_LIMIT),
        name="inproj",
    )(x, g, w)


def _attn_kernel(lq1_ref, lk1_ref, lq2_ref, lk2_ref, sub_ref, q_ref, k_ref, v_ref, o_ref,
                 qt_s, ka_s, vt_s, m_s, l_s, acc_s, *, lam_init):
    bq = ATT_BLOCK
    hp = ATT_HEADS_PER_STEP
    n_blocks = k_ref.shape[1] // bq
    g = pl.program_id(1)
    i = pl.program_id(2)

    slopes = []
    for hh in range(hp):
        h = g * hp + hh
        slope = jnp.float32(0.0)
        for hv in range(DIFF_HEADS):
            slope = jnp.where(h == hv, ALIBI_SLOPES_LOG2[hv], slope)
        slopes.append(slope.astype(F32))

    @pl.when(i == 0)
    def _():
        key_pos = lax.broadcasted_iota(jnp.int32, (bq, LANES), 0)
        lane = lax.broadcasted_iota(jnp.int32, (bq, LANES), 1)
        for hh in range(hp):
            for jb in range(n_blocks):
                rows = slice(jb * bq, (jb + 1) * bq)
                v = v_ref[0, rows, hh * LANES:(hh + 1) * LANES]
                vt_s[hh, jb] = v.astype(F32).T.astype(BF16)
                x = slopes[hh] * (key_pos + jb * bq).astype(F32)
                hi = x.astype(BF16).astype(F32)
                mid = (x - hi).astype(BF16).astype(F32)
                lo = x - hi - mid
                ka_s[hh, rows, 0:LANES] = k_ref[0, rows, hh * LANES:(hh + 1) * LANES]
                ka_s[hh, rows, LANES:2 * LANES] = jnp.where(
                    lane == 0, hi, jnp.where(lane == 1, mid, jnp.where(lane == 2, lo, 0.0))
                ).astype(BF16)

    row = lax.broadcasted_iota(jnp.int32, (LANES, bq), 0)
    half = row < DIFF_QK_DIM
    ones_rows = jnp.where(row < 3, 1.0, 0.0).astype(BF16)
    for hh in range(hp):
        qt = q_ref[0, :, hh * LANES:(hh + 1) * LANES].astype(F32).T
        qt_s[hh, 0:LANES, 0:bq] = jnp.where(half, qt, 0.0).astype(BF16)
        qt_s[hh, 0:LANES, bq:2 * bq] = jnp.where(half, 0.0, qt).astype(BF16)
        qt_s[hh, LANES:2 * LANES, 0:bq] = ones_rows
        qt_s[hh, LANES:2 * LANES, bq:2 * bq] = ones_rows

    qw = MXU_COLS
    units = [(hh, c * qw) for hh in range(hp) for c in range(2 * bq // qw)]

    def keys_needed(unit, diagonal):
        return unit[1] % bq + qw if diagonal else bq

    def scores(unit, j, diagonal):
        hh, c0 = unit
        off = pl.multiple_of(j * bq, bq)
        nk = keys_needed(unit, diagonal)
        return _dot(ka_s[hh, pl.ds(off, nk), :], qt_s[hh, :, c0:c0 + qw])

    def for_each_unit(j, consume, diagonal=False):
        pending = [scores(u, j, diagonal) for u in units[:ATT_SCORE_LOOKAHEAD]]
        for n, unit in enumerate(units):
            if n + ATT_SCORE_LOOKAHEAD < len(units):
                pending.append(scores(units[n + ATT_SCORE_LOOKAHEAD], j, diagonal))
            consume(unit, pending.pop(0))

    def first_block(unit, s):
        hh, c0 = unit
        nk = keys_needed(unit, True)
        key_idx = lax.broadcasted_iota(jnp.int32, (nk, qw), 0)
        col_idx = lax.broadcasted_iota(jnp.int32, (nk, qw), 1)
        s = jnp.where(key_idx <= col_idx + c0 % bq, s, -jnp.inf)
        m0 = jnp.max(s, axis=0, keepdims=True)
        p = jnp.exp2(s - m0)
        m_s[hh, :, c0:c0 + qw] = m0
        l_s[hh, :, c0:c0 + qw] = jnp.sum(p, axis=0, keepdims=True)
        acc_s[hh, :, c0:c0 + qw] = _dot(vt_s[hh, i, :, 0:nk], p.astype(BF16))

    for_each_unit(i, first_block, diagonal=True)

    def body(j, carry):
        def update(unit, s):
            hh, c0 = unit
            m_prev = m_s[hh, :, c0:c0 + qw]
            m_new = jnp.maximum(m_prev, jnp.max(s, axis=0, keepdims=True))
            alpha = jnp.exp2(m_prev - m_new)
            p = jnp.exp2(s - m_new)
            l_s[hh, :, c0:c0 + qw] = alpha * l_s[hh, :, c0:c0 + qw] + jnp.sum(p, axis=0, keepdims=True)
            acc_s[hh, :, c0:c0 + qw] = (alpha * acc_s[hh, :, c0:c0 + qw]
                                        + _dot(vt_s[hh, j], p.astype(BF16)))
            m_s[hh, :, c0:c0 + qw] = m_new
        for_each_unit(j, update)
        return carry

    lax.fori_loop(0, i, body, 0)

    lam = (jnp.exp(jnp.sum(lq1_ref[...] * lk1_ref[...], axis=-1, keepdims=True))
           - jnp.exp(jnp.sum(lq2_ref[...] * lk2_ref[...], axis=-1, keepdims=True))
           + lam_init)
    for hh in range(hp):
        on = acc_s[hh] * (1.0 / l_s[hh])
        ot = on[:, 0:bq] - lam * on[:, bq:2 * bq]
        ot = ot * lax.rsqrt(jnp.mean(ot * ot, axis=0, keepdims=True) + NORM_EPS)
        o_ref[0, :, hh * LANES:(hh + 1) * LANES] = (
            ot.T * sub_ref[...] * (1.0 - lam_init)).astype(BF16)


def _attn_call(qkv, lq1, lk1, lq2, lk2, sub, lam_init):
    b, s, _ = qkv.shape
    bq = ATT_BLOCK
    hp = ATT_HEADS_PER_STEP
    w = hp * LANES
    vec = lambda n: pl.BlockSpec((1, n), lambda bi, gi, qi: (0, 0))
    return pl.pallas_call(
        functools.partial(_attn_kernel, lam_init=lam_init),
        grid=(b, DIFF_HEADS // hp, s // bq),
        in_specs=[vec(DIFF_QK_DIM), vec(DIFF_QK_DIM), vec(DIFF_QK_DIM), vec(DIFF_QK_DIM),
                  vec(DIFF_V_DIM),
                  pl.BlockSpec((1, bq, w), lambda bi, gi, qi: (bi, qi, gi)),
                  pl.BlockSpec((1, s, w), lambda bi, gi, qi: (bi, 0, DIFF_QK_W // w + gi)),
                  pl.BlockSpec((1, s, w), lambda bi, gi, qi: (bi, 0, 2 * DIFF_QK_W // w + gi))],
        out_specs=pl.BlockSpec((1, bq, w), lambda bi, gi, qi: (bi, qi, gi)),
        out_shape=jax.ShapeDtypeStruct((b, s, DIFF_V_W), BF16),
        scratch_shapes=[pltpu.VMEM((hp, 2 * LANES, 2 * bq), BF16),
                        pltpu.VMEM((hp, s, 2 * LANES), BF16),
                        pltpu.VMEM((hp, s // bq, DIFF_V_DIM, bq), BF16),
                        pltpu.VMEM((hp, 1, 2 * bq), F32),
                        pltpu.VMEM((hp, 1, 2 * bq), F32),
                        pltpu.VMEM((hp, DIFF_V_DIM, 2 * bq), F32)],
        compiler_params=pltpu.CompilerParams(
            dimension_semantics=("arbitrary", "arbitrary", "arbitrary"),
            vmem_limit_bytes=VMEM_LIMIT),
        name="diff_attn",
    )(lq1, lk1, lq2, lk2, sub, qkv, qkv, qkv)


def _ret_kernel(q_ref, k_ref, v_ref, g_ref, o_ref, kv_s, prev_s, dec_s):
    c = RET_CHUNK
    n_chunks = q_ref.shape[1] // c
    r = lax.broadcasted_iota(jnp.int32, (c, c), 0)
    cc = lax.broadcasted_iota(jnp.int32, (c, c), 1)
    rel = (r - cc).astype(F32)
    rf = r.astype(F32)
    for h in range(RET_HEADS):
        lg = RET_LOG_DECAY[h]
        dec_s[h, 0] = jnp.where(rel >= 0, jnp.exp(lg * rel), 0.0)
        dec_s[h, 1] = jnp.exp(lg * (c - 1.0 - rf))
        dec_s[h, 2] = jnp.exp(lg * (rf + 1.0))
    upper = cc >= RET_K_DIM

    def head_slab(ref, off, p, e):
        slab = ref[0, pl.ds(off, c), p * LANES:(p + 1) * LANES].astype(F32)
        return jnp.where(upper if e else jnp.logical_not(upper), slab, 0.0)

    def kv_body(n, carry):
        off = pl.multiple_of(n * c, c)
        for h in range(RET_HEADS):
            kh = head_slab(k_ref, off, h // 2, h % 2) * (RET_K_DIM ** -0.5)
            v = v_ref[0, pl.ds(off, c), h * RET_V_DIM:(h + 1) * RET_V_DIM]
            kv_s[h, n] = _dot_tn((kh * dec_s[h, 1]).astype(BF16), v)
        return carry

    lax.fori_loop(0, n_chunks, kv_body, 0, unroll=RET_UNROLL)

    for h in range(RET_HEADS):
        state = jnp.zeros((c, RET_V_DIM), F32)
        for n in range(n_chunks):
            prev_s[h, n] = state.astype(BF16)
            state = state * math.exp(RET_LOG_DECAY[h] * c) + kv_s[h, n]

    def out_body(n, carry):
        off = pl.multiple_of(n * c, c)
        qhs = [head_slab(q_ref, off, h // 2, h % 2) for h in range(RET_HEADS)]
        scs = [_dot_nt(qhs[h].astype(BF16),
                       (head_slab(k_ref, off, h // 2, h % 2) * (RET_K_DIM ** -0.5)).astype(BF16))
               for h in range(RET_HEADS)]
        for h in range(RET_HEADS):
            qh = qhs[h]
            v = v_ref[0, pl.ds(off, c), h * RET_V_DIM:(h + 1) * RET_V_DIM]
            sc = scs[h] * dec_s[h, 0]
            lhs = jnp.concatenate([sc.astype(BF16), (qh * dec_s[h, 2]).astype(BF16)], axis=1)
            rhs = jnp.concatenate([v, prev_s[h, n]], axis=0)
            o = _dot(lhs, rhs)
            o = o * lax.rsqrt(jnp.mean(o * o, axis=-1, keepdims=True) + NORM_EPS)
            gate = g_ref[0, pl.ds(off, c), h * RET_V_DIM:(h + 1) * RET_V_DIM]
            o_ref[0, pl.ds(off, c), h * RET_V_DIM:(h + 1) * RET_V_DIM] = (
                jax.nn.silu(gate) * o).astype(BF16)
        return carry

    lax.fori_loop(0, n_chunks, out_body, 0, unroll=RET_UNROLL)


def _ret_call(qkv, rg):
    b, s, _ = qkv.shape
    return pl.pallas_call(
        _ret_kernel,
        grid=(b,),
        in_specs=[pl.BlockSpec((1, s, RET_QK_W), lambda bi: (bi, 0, RQ_BLK)),
                  pl.BlockSpec((1, s, RET_QK_W), lambda bi: (bi, 0, RK_BLK)),
                  pl.BlockSpec((1, s, RET_V_W), lambda bi: (bi, 0, RV_BLK)),
                  pl.BlockSpec((1, s, RET_V_W), lambda bi: (bi, 0, 0))],
        out_specs=pl.BlockSpec((1, s, RET_V_W), lambda bi: (bi, 0, 0)),
        out_shape=jax.ShapeDtypeStruct((b, s, RET_V_W), BF16),
        scratch_shapes=[pltpu.VMEM((RET_HEADS, s // RET_CHUNK, LANES, RET_V_DIM), F32),
                        pltpu.VMEM((RET_HEADS, s // RET_CHUNK, LANES, RET_V_DIM), BF16),
                        pltpu.VMEM((RET_HEADS, 3, RET_CHUNK, RET_CHUNK), F32)],
        compiler_params=pltpu.CompilerParams(
            dimension_semantics=("arbitrary",), vmem_limit_bytes=VMEM_LIMIT),
        name="retention",
    )(qkv, qkv, qkv, rg)


def kernel(x, norm_ffn1, w1_gate, w1_up, w1_down, norm_mix, w_in, lambda_q1, lambda_k1, lambda_q2, lambda_k2, diff_subln, w_out, norm_ffn2, w2_gate, w2_up, w2_down, norm_final):
    b, s, d = x.shape
    t = b * s
    xt = x.reshape(t, d)
    row = lambda a: a.reshape(1, -1).astype(F32)
    for l in range(DEPTH):
        lam_init = 0.8 - 0.6 * math.exp(-0.3 * l)
        xt = _ffn_call(xt, row(norm_ffn1[l]), w1_gate[l].astype(BF16), w1_up[l].astype(BF16),
                       w1_down[l].astype(BF16))
        qkv, rg = _inproj_call(xt, row(norm_mix[l]), w_in[l].astype(BF16))
        qkv = qkv.reshape(b, s, QKV_COLS)
        od = _attn_call(qkv, row(lambda_q1[l]), row(lambda_k1[l]), row(lambda_q2[l]),
                        row(lambda_k2[l]), row(diff_subln[l]), lam_init)
        orr = _ret_call(qkv, rg.reshape(b, s, RET_V_W))
        wo = w_out[l].astype(BF16)
        xt = _mix_ffn_call(xt, od.reshape(t, DIFF_V_W), orr.reshape(t, RET_V_W),
                           wo[:DIFF_V_W], wo[DIFF_V_W:], row(norm_ffn2[l]),
                           w2_gate[l].astype(BF16), w2_up[l].astype(BF16),
                           w2_down[l].astype(BF16), row(norm_final), l == DEPTH - 1)
    return xt.reshape(b, s, d)
```

```python
import functools
import math

import numpy as np
import jax
import jax.numpy as jnp
from jax import lax
from jax.experimental import pallas as pl
from jax.experimental.pallas import tpu as pltpu

F32 = jnp.float32
BF16 = jnp.bfloat16

D_MODEL = 1024
DEPTH = 2
DIFF_HEADS = 4
DIFF_QK_DIM = 64
DIFF_V_DIM = 128
RET_HEADS = 4
RET_K_DIM = 64
RET_V_DIM = 128
D_FF = 2816
RET_CHUNK = 128
NORM_EPS = 1e-6
HALF_STEP = 0.5

DIFF_QK_W = DIFF_HEADS * 2 * DIFF_QK_DIM
DIFF_V_W = DIFF_HEADS * DIFF_V_DIM
RET_QK_W = RET_HEADS * RET_K_DIM
RET_V_W = RET_HEADS * RET_V_DIM
QKV_COLS = 2 * DIFF_QK_W + DIFF_V_W + 2 * RET_QK_W + RET_V_W
IN_COLS = QKV_COLS + RET_V_W

LANES = 128
MXU_COLS = 256

RQ_BLK = (2 * DIFF_QK_W + DIFF_V_W) // RET_QK_W
RK_BLK = RQ_BLK + 1
RV_BLK = (2 * DIFF_QK_W + DIFF_V_W + 2 * RET_QK_W) // RET_V_W

TOKEN_TILE = 1024
FF_CHUNKS = ((0, 1024), (1024, 2048), (2048, 2816))
ATT_SCORE_LOOKAHEAD = 4
ALIBI_PIECES = 3
VMEM_LIMIT = 56 * 1024 * 1024

RET_LOG_DECAY = [float(v) for v in
                 np.log(1.0 - 2.0 ** (-5.0 - np.arange(RET_HEADS))).astype(np.float32)]
LOG2_E = math.log2(math.e)
DIFF_Q_SCALE = DIFF_QK_DIM ** -0.5 * LOG2_E
ALIBI_SLOPES_LOG2 = [float(2.0 ** (-8.0 * (h + 1) / DIFF_HEADS)) * LOG2_E
                     for h in range(DIFF_HEADS)]


def _rms(x, g):
    ms = jnp.mean(x * x, axis=-1, keepdims=True)
    return x * lax.rsqrt(ms + NORM_EPS) * g


def _dot(a, b):
    return jnp.dot(a, b, preferred_element_type=F32)


def _dot_nt(a, b):
    return lax.dot_general(a, b, (((1,), (1,)), ((), ())), preferred_element_type=F32)


def _dot_tn(a, b):
    return lax.dot_general(a, b, (((0,), (0,)), ((), ())), preferred_element_type=F32)


def _swiglu_residual(x, g_ref, wg_ref, wu_ref, wd_ref):
    xn = _rms(x, g_ref[...]).astype(BF16)
    y = None
    for c0, c1 in FF_CHUNKS:
        gate = _dot(xn, wg_ref[:, c0:c1])
        up = _dot(xn, wu_ref[:, c0:c1])
        h = (jax.nn.silu(gate) * up).astype(BF16)
        part = _dot(h, wd_ref[c0:c1, :])
        y = part if y is None else y + part
    return x + HALF_STEP * y


def _ffn_kernel(x_ref, g_ref, wg_ref, wu_ref, wd_ref, o_ref):
    o_ref[...] = _swiglu_residual(x_ref[...], g_ref, wg_ref, wu_ref, wd_ref)


def _mix_ffn_kernel(x_ref, od_ref, or_ref, woa_ref, wob_ref, g_ref, wg_ref, wu_ref, wd_ref,
                    gf_ref, o_ref, *, final_norm):
    x = x_ref[...] + _dot(od_ref[...], woa_ref[...]) + _dot(or_ref[...], wob_ref[...])
    x = _swiglu_residual(x, g_ref, wg_ref, wu_ref, wd_ref)
    if final_norm:
        x = _rms(x, gf_ref[...])
    o_ref[...] = x


def _inproj_kernel(x_ref, g_ref, w_ref, qkv_ref, rg_ref):
    xn = _rms(x_ref[...], g_ref[...]).astype(BF16)
    qkv_ref[:, :DIFF_QK_W] = (_dot(xn, w_ref[:, :DIFF_QK_W]) * DIFF_Q_SCALE).astype(BF16)
    qkv_ref[:, DIFF_QK_W:] = _dot(xn, w_ref[:, DIFF_QK_W:QKV_COLS]).astype(BF16)
    rg_ref[...] = _dot(xn, w_ref[:, QKV_COLS:])


def _alibi_kernel(o_ref):
    assert ALIBI_PIECES == 3
    seq = o_ref.shape[0]
    pos = lax.broadcasted_iota(jnp.int32, (seq, LANES), 0).astype(F32)
    lane = lax.broadcasted_iota(jnp.int32, (seq, LANES), 1)
    for h in range(DIFF_HEADS):
        x = ALIBI_SLOPES_LOG2[h] * pos
        hi = x.astype(BF16).astype(F32)
        mid = (x - hi).astype(BF16).astype(F32)
        lo = x - hi - mid
        o_ref[:, h * LANES:(h + 1) * LANES] = jnp.where(
            lane == 0, hi, jnp.where(lane == 1, mid, jnp.where(lane == 2, lo, 0.0))).astype(BF16)


def _alibi_call(seq):
    return pl.pallas_call(
        _alibi_kernel,
        out_shape=jax.ShapeDtypeStruct((seq, DIFF_HEADS * LANES), BF16),
        name="alibi_cols",
    )()


def _resident(shape):
    return pl.BlockSpec(shape, lambda *_: (0,) * len(shape), pipeline_mode=pl.Buffered(1))


def _layer_resident(layer, shape, block=0):
    return pl.BlockSpec((None,) + shape, lambda *_: (layer, block, 0),
                        pipeline_mode=pl.Buffered(1))


def _ffn_call(layer, x, g, wg, wu, wd):
    t = x.shape[0]
    tile = pl.BlockSpec((TOKEN_TILE, D_MODEL), lambda i: (i, 0))
    return pl.pallas_call(
        _ffn_kernel,
        grid=(t // TOKEN_TILE,),
        in_specs=[tile, _resident((1, D_MODEL)), _layer_resident(layer, (D_MODEL, D_FF)),
                  _layer_resident(layer, (D_MODEL, D_FF)),
                  _layer_resident(layer, (D_FF, D_MODEL))],
        out_specs=tile,
        out_shape=jax.ShapeDtypeStruct((t, D_MODEL), F32),
        compiler_params=pltpu.CompilerParams(
            dimension_semantics=("arbitrary",), vmem_limit_bytes=VMEM_LIMIT),
        name="ffn",
    )(x, g, wg, wu, wd)


def _mix_ffn_call(layer, x, od, orr, wo, g, wg, wu, wd, gf, final_norm):
    t = x.shape[0]
    tile = pl.BlockSpec((TOKEN_TILE, D_MODEL), lambda i: (i, 0))
    half = pl.BlockSpec((TOKEN_TILE, DIFF_V_W), lambda i: (i, 0))
    assert DIFF_V_W == RET_V_W
    return pl.pallas_call(
        functools.partial(_mix_ffn_kernel, final_norm=final_norm),
        grid=(t // TOKEN_TILE,),
        in_specs=[tile, half, half,
                  _layer_resident(layer, (DIFF_V_W, D_MODEL), block=0),
                  _layer_resident(layer, (RET_V_W, D_MODEL), block=1),
                  _resident((1, D_MODEL)), _layer_resident(layer, (D_MODEL, D_FF)),
                  _layer_resident(layer, (D_MODEL, D_FF)),
                  _layer_resident(layer, (D_FF, D_MODEL)),
                  _resident((1, D_MODEL))],
        out_specs=tile,
        out_shape=jax.ShapeDtypeStruct((t, D_MODEL), F32),
        compiler_params=pltpu.CompilerParams(
            dimension_semantics=("arbitrary",), vmem_limit_bytes=VMEM_LIMIT),
        name="mix_ffn",
    )(x, od, orr, wo, wo, g, wg, wu, wd, gf)


def _inproj_call(layer, x, g, w):
    t = x.shape[0]
    return pl.pallas_call(
        _inproj_kernel,
        grid=(t // TOKEN_TILE,),
        in_specs=[pl.BlockSpec((TOKEN_TILE, D_MODEL), lambda i: (i, 0)),
                  _resident((1, D_MODEL)), _layer_resident(layer, (D_MODEL, IN_COLS))],
        out_specs=[pl.BlockSpec((TOKEN_TILE, QKV_COLS), lambda i: (i, 0)),
                   pl.BlockSpec((TOKEN_TILE, RET_V_W), lambda i: (i, 0))],
        out_shape=[jax.ShapeDtypeStruct((t, QKV_COLS), BF16),
                   jax.ShapeDtypeStruct((t, RET_V_W), F32)],
        compiler_params=pltpu.CompilerParams(
            dimension_semantics=("arbitrary",), vmem_limit_bytes=VMEM_LIMIT),
        name="inproj",
    )(x, g, w)


def _attn_kernel(lq1_ref, lk1_ref, lq2_ref, lk2_ref, sub_ref, ab_ref, q_ref, k_ref, v_ref, o_ref,
                 qt_s, *, lam_init):
    seq = q_ref.shape[1]
    qw = MXU_COLS
    n_slabs = seq // qw

    row = lax.broadcasted_iota(jnp.int32, (LANES, qw), 0)
    half = row < DIFF_QK_DIM
    ones_rows = jnp.where(row < ALIBI_PIECES, 1.0, 0.0).astype(BF16)
    for hh in range(DIFF_HEADS):
        qt = q_ref[0, :, hh * LANES:(hh + 1) * LANES].astype(F32).T
        for c in range(n_slabs):
            slab = qt[:, c * qw:(c + 1) * qw]
            qt_s[hh, c, 0:LANES, 0:qw] = jnp.where(half, slab, 0.0).astype(BF16)
            qt_s[hh, c, 0:LANES, qw:2 * qw] = jnp.where(half, 0.0, slab).astype(BF16)
            qt_s[hh, c, LANES:2 * LANES, 0:qw] = ones_rows
            qt_s[hh, c, LANES:2 * LANES, qw:2 * qw] = ones_rows

    lam = (jnp.exp(jnp.sum(lq1_ref[...] * lk1_ref[...], axis=-1, keepdims=True))
           - jnp.exp(jnp.sum(lq2_ref[...] * lk2_ref[...], axis=-1, keepdims=True))
           + lam_init)
    gain = sub_ref[...] * (1.0 - lam_init)
    q_idx = lax.broadcasted_iota(jnp.int32, (qw, 2 * qw), 1)
    q_idx = jnp.where(q_idx >= qw, q_idx - qw, q_idx)
    causal = lax.broadcasted_iota(jnp.int32, (qw, 2 * qw), 0) <= q_idx

    jobs = [(hh, c) for hh in range(DIFF_HEADS) for c in range(n_slabs)]

    def scores(job):
        hh, c = job
        nk = (c + 1) * qw
        ka = jnp.concatenate([k_ref[0, 0:nk, hh * LANES:(hh + 1) * LANES],
                              ab_ref[0:nk, hh * LANES:(hh + 1) * LANES]], axis=1)
        return _dot(ka, qt_s[hh, c])

    def finish(job, s):
        hh, c = job
        nk = (c + 1) * qw
        tail = jnp.where(causal, s[nk - qw:, :], -jnp.inf)
        s = tail if nk == qw else jnp.concatenate([s[:nk - qw, :], tail], axis=0)
        p = jnp.exp2(s - jnp.max(s, axis=0, keepdims=True))
        inv = 1.0 / jnp.sum(p, axis=0, keepdims=True)
        v = v_ref[0, 0:nk, hh * DIFF_V_DIM:(hh + 1) * DIFF_V_DIM]
        on = _dot_tn(v, p.astype(BF16)) * inv
        ot = on[:, 0:qw] - lam * on[:, qw:2 * qw]
        ot = ot * lax.rsqrt(jnp.mean(ot * ot, axis=0, keepdims=True) + NORM_EPS)
        o_ref[0, c * qw:(c + 1) * qw, hh * LANES:(hh + 1) * LANES] = (ot.T * gain).astype(BF16)

    pending = [scores(job) for job in jobs[:ATT_SCORE_LOOKAHEAD]]
    for n, job in enumerate(jobs):
        if n + ATT_SCORE_LOOKAHEAD < len(jobs):
            pending.append(scores(jobs[n + ATT_SCORE_LOOKAHEAD]))
        finish(job, pending.pop(0))


def _attn_call(qkv, alibi, lq1, lk1, lq2, lk2, sub, lam_init):
    b, s, _ = qkv.shape
    hp = DIFF_HEADS
    w = DIFF_QK_W
    assert DIFF_V_W == w
    vec = lambda n: pl.BlockSpec((1, n), lambda bi: (0, 0))
    part = lambda blk: pl.BlockSpec((1, s, w), lambda bi: (bi, 0, blk))
    return pl.pallas_call(
        functools.partial(_attn_kernel, lam_init=lam_init),
        grid=(b,),
        in_specs=[vec(DIFF_QK_DIM), vec(DIFF_QK_DIM), vec(DIFF_QK_DIM), vec(DIFF_QK_DIM),
                  vec(DIFF_V_DIM), _resident((s, hp * LANES)), part(0), part(1), part(2)],
        out_specs=pl.BlockSpec((1, s, DIFF_V_W), lambda bi: (bi, 0, 0)),
        out_shape=jax.ShapeDtypeStruct((b, s, DIFF_V_W), BF16),
        scratch_shapes=[pltpu.VMEM((hp, s // MXU_COLS, 2 * LANES, 2 * MXU_COLS), BF16)],
        compiler_params=pltpu.CompilerParams(
            dimension_semantics=("arbitrary",), vmem_limit_bytes=VMEM_LIMIT),
        name="diff_attn",
    )(lq1, lk1, lq2, lk2, sub, alibi, qkv, qkv, qkv)


def _ret_kernel(q_ref, k_ref, v_ref, g_ref, o_ref, kv_s, prev_s, dec_s):
    c = RET_CHUNK
    n_chunks = q_ref.shape[1] // c
    r = lax.broadcasted_iota(jnp.int32, (c, c), 0)
    cc = lax.broadcasted_iota(jnp.int32, (c, c), 1)
    rel = (r - cc).astype(F32)
    rf = r.astype(F32)
    k_scale = RET_K_DIM ** -0.5
    for h in range(RET_HEADS):
        lg = RET_LOG_DECAY[h]
        dec_s[h, 0] = jnp.where(rel >= 0, jnp.exp(lg * rel), 0.0) * k_scale
        dec_s[h, 1] = jnp.exp(lg * (c - 1.0 - rf)) * k_scale
        dec_s[h, 2] = jnp.exp(lg * (rf + 1.0))
    upper = cc >= RET_K_DIM

    def head_slab(ref, off, p, e):
        slab = ref[0, pl.ds(off, c), p * LANES:(p + 1) * LANES].astype(F32)
        return jnp.where(upper if e else jnp.logical_not(upper), slab, 0.0)

    for n in range(n_chunks):
        off = n * c
        for h in range(RET_HEADS):
            kp = k_ref[0, pl.ds(off, c), (h // 2) * LANES:(h // 2 + 1) * LANES].astype(F32)
            v = v_ref[0, pl.ds(off, c), h * RET_V_DIM:(h + 1) * RET_V_DIM]
            kv_s[h, n] = _dot_tn((kp * dec_s[h, 1]).astype(BF16), v)

    for h in range(RET_HEADS):
        state = jnp.zeros((c, RET_V_DIM), F32)
        for n in range(n_chunks):
            prev_s[h, n] = state.astype(BF16)
            state = state * math.exp(RET_LOG_DECAY[h] * c) + kv_s[h, n]

    for n in range(n_chunks):
        off = n * c
        qhs = [head_slab(q_ref, off, h // 2, h % 2) for h in range(RET_HEADS)]
        scs = [_dot_nt(qhs[h].astype(BF16),
                       k_ref[0, pl.ds(off, c), (h // 2) * LANES:(h // 2 + 1) * LANES])
               for h in range(RET_HEADS)]
        for h in range(RET_HEADS):
            qh = qhs[h]
            v = v_ref[0, pl.ds(off, c), h * RET_V_DIM:(h + 1) * RET_V_DIM]
            sc = scs[h] * dec_s[h, 0]
            lhs = jnp.concatenate([sc.astype(BF16), (qh * dec_s[h, 2]).astype(BF16)], axis=1)
            rhs = jnp.concatenate([v, prev_s[h, n]], axis=0)
            o = _dot(lhs, rhs)
            o = o * lax.rsqrt(jnp.mean(o * o, axis=-1, keepdims=True) + NORM_EPS)
            gate = g_ref[0, pl.ds(off, c), h * RET_V_DIM:(h + 1) * RET_V_DIM]
            o_ref[0, pl.ds(off, c), h * RET_V_DIM:(h + 1) * RET_V_DIM] = (
                jax.nn.silu(gate) * o).astype(BF16)


def _ret_call(qkv, rg):
    b, s, _ = qkv.shape
    return pl.pallas_call(
        _ret_kernel,
        grid=(b,),
        in_specs=[pl.BlockSpec((1, s, RET_QK_W), lambda bi: (bi, 0, RQ_BLK)),
                  pl.BlockSpec((1, s, RET_QK_W), lambda bi: (bi, 0, RK_BLK)),
                  pl.BlockSpec((1, s, RET_V_W), lambda bi: (bi, 0, RV_BLK)),
                  pl.BlockSpec((1, s, RET_V_W), lambda bi: (bi, 0, 0))],
        out_specs=pl.BlockSpec((1, s, RET_V_W), lambda bi: (bi, 0, 0)),
        out_shape=jax.ShapeDtypeStruct((b, s, RET_V_W), BF16),
        scratch_shapes=[pltpu.VMEM((RET_HEADS, s // RET_CHUNK, LANES, RET_V_DIM), F32),
                        pltpu.VMEM((RET_HEADS, s // RET_CHUNK, LANES, RET_V_DIM), BF16),
                        pltpu.VMEM((RET_HEADS, 3, RET_CHUNK, RET_CHUNK), F32)],
        compiler_params=pltpu.CompilerParams(
            dimension_semantics=("arbitrary",), vmem_limit_bytes=VMEM_LIMIT),
        name="retention",
    )(qkv, qkv, qkv, rg)


def kernel(x, norm_ffn1, w1_gate, w1_up, w1_down, norm_mix, w_in, lambda_q1, lambda_k1, lambda_q2, lambda_k2, diff_subln, w_out, norm_ffn2, w2_gate, w2_up, w2_down, norm_final):
    b, s, d = x.shape
    t = b * s
    xt = x.reshape(t, d)
    row = lambda a: a.reshape(1, -1).astype(F32)
    alibi = _alibi_call(s)
    w1g, w1u, w1d = w1_gate.astype(BF16), w1_up.astype(BF16), w1_down.astype(BF16)
    w2g, w2u, w2d = w2_gate.astype(BF16), w2_up.astype(BF16), w2_down.astype(BF16)
    wi, wo = w_in.astype(BF16), w_out.astype(BF16)
    for l in range(DEPTH):
        lam_init = 0.8 - 0.6 * math.exp(-0.3 * l)
        xt = _ffn_call(l, xt, row(norm_ffn1[l]), w1g, w1u, w1d)
        qkv, rg = _inproj_call(l, xt, row(norm_mix[l]), wi)
        qkv = qkv.reshape(b, s, QKV_COLS)
        od = _attn_call(qkv, alibi, row(lambda_q1[l]), row(lambda_k1[l]), row(lambda_q2[l]),
                        row(lambda_k2[l]), row(diff_subln[l]), lam_init)
        orr = _ret_call(qkv, rg.reshape(b, s, RET_V_W))
        xt = _mix_ffn_call(l, xt, od.reshape(t, DIFF_V_W), orr.reshape(t, RET_V_W), wo,
                           row(norm_ffn2[l]), w2g, w2u, w2d, row(norm_final), l == DEPTH - 1)
    return xt.reshape(b, s, d)
```
